```python
import jax, jax.numpy as jnp
from jax import lax
import numpy as np

D_MODEL = 1024
BATCH = 8
SEQ = 8192
DEPTH = 2
DEC_BATCH = 8
DEC_SEQ = 16
PAST_LEN = 2048

CHUNK = 64
QBLOCK = 128
N_HEADS = 8
QK_NOPE_DIM = 128
QK_ROPE_DIM = 64
V_HEAD_DIM = 128
Q_LORA_RANK = 384
KV_LORA_RANK = 256
ROPE_THETA = 10000.0
ATTN_SCALE = (QK_NOPE_DIM + QK_ROPE_DIM) ** -0.5
CONV_CH = 1024
CONV_WIDTH = 31
D_FF = 2816
FFN_CONV_WIDTH = 3
EPS = 1e-6
IN_COLS = Q_LORA_RANK + KV_LORA_RANK + QK_ROPE_DIM + 2 * CONV_CH + 2 * D_MODEL
IN_SPLITS = [Q_LORA_RANK, Q_LORA_RANK + KV_LORA_RANK, Q_LORA_RANK + KV_LORA_RANK + QK_ROPE_DIM,
             Q_LORA_RANK + KV_LORA_RANK + QK_ROPE_DIM + 2 * CONV_CH]

kernel_name = 'streaming_mla_conformer_hybrid_step'


def rms_norm(x, g):
    xf = x.astype(jnp.float32)
    y = xf * lax.rsqrt(jnp.mean(xf * xf, axis=-1, keepdims=True) + EPS)
    return (y * g.astype(jnp.float32)).astype(x.dtype)


def layer_norm(x, g, b):
    xf = x.astype(jnp.float32)
    mu = jnp.mean(xf, axis=-1, keepdims=True)
    d = xf - mu
    y = d * lax.rsqrt(jnp.mean(d * d, axis=-1, keepdims=True) + EPS)
    return (y * g.astype(jnp.float32) + b.astype(jnp.float32)).astype(x.dtype)


def rope_tables(pos):
    half = QK_ROPE_DIM // 2
    inv = jnp.power(ROPE_THETA, -jnp.arange(half, dtype=jnp.float32) / half)
    ang = pos.astype(jnp.float32)[:, None] * inv[None, :]
    return jnp.cos(ang), jnp.sin(ang)


def apply_rope(x, cos, sin):
    xf = x.astype(jnp.float32)
    x1, x2 = jnp.split(xf, 2, axis=-1)
    out = jnp.concatenate([x1 * cos - x2 * sin, x2 * cos + x1 * sin], axis=-1)
    return out.astype(x.dtype)


def causal_dwconv(x, prev, w, b):
    xp = jnp.concatenate([prev.astype(x.dtype), x], axis=1)
    y = lax.conv_general_dilated(xp, w[:, None, :].astype(x.dtype), window_strides=(1,), padding='VALID',
                                 dimension_numbers=('NWC', 'WIO', 'NWC'), feature_group_count=x.shape[-1])
    return y + b.astype(x.dtype), xp[:, xp.shape[1] - (w.shape[0] - 1):]


def mla_attend(q_lat, q_pe, c_kv, k_pe, mask):
    s = jnp.einsum('bqhc,bsc->bhqs', q_lat, c_kv) + jnp.einsum('bqhr,bsr->bhqs', q_pe, k_pe)
    s = s.astype(jnp.float32) * ATTN_SCALE
    if mask is not None:
        s = jnp.where(mask[None, None], s, -jnp.inf)
    p = jax.nn.softmax(s, axis=-1).astype(c_kv.dtype)
    return jnp.einsum('bhqs,bsc->bqhc', p, c_kv)


def prompt_attention(q_lat, q_pe, c_kv, k_pe):
    B, T = q_lat.shape[0], q_lat.shape[1]
    nb = T // QBLOCK
    ql = q_lat.reshape(B, nb, QBLOCK, N_HEADS, KV_LORA_RANK).transpose(1, 0, 2, 3, 4)
    qp = q_pe.reshape(B, nb, QBLOCK, N_HEADS, QK_ROPE_DIM).transpose(1, 0, 2, 3, 4)
    key_chunk = jnp.arange(T) // CHUNK

    def block(args):
        qlb, qpb, i = args
        q_chunk = (i * QBLOCK + jnp.arange(QBLOCK)) // CHUNK
        mask = key_chunk[None, :] <= q_chunk[:, None]
        return mla_attend(qlb, qpb, c_kv, k_pe, mask)

    out = lax.map(block, (ql, qp, jnp.arange(nb)))
    return out.transpose(1, 0, 2, 3, 4).reshape(B, T, N_HEADS, KV_LORA_RANK)


def hybrid_layer(x, pos, cache_lat, cache_kpe, conv_prev, ffn_prev,
                 norm_mix, w_in, q_norm, w_qb, kv_norm, w_kvb, w_o_attn,
                 conv_dw, conv_dw_b, conv_ln_g, conv_ln_b, w_conv_out, w_out,
                 norm_ffn, w_up, ffn_dw, ffn_dw_b, w_down):
    B, T, _ = x.shape
    h = rms_norm(x, norm_mix)
    q_a, kv_a, k_pe, conv_in, gates = jnp.split(h @ w_in, IN_SPLITS, axis=-1)

    cos, sin = rope_tables(pos)
    q = (rms_norm(q_a, q_norm) @ w_qb).reshape(B, T, N_HEADS, QK_NOPE_DIM + QK_ROPE_DIM)
    q_nope, q_pe = q[..., :QK_NOPE_DIM], q[..., QK_NOPE_DIM:]
    q_pe = apply_rope(q_pe, cos[None, :, None], sin[None, :, None])
    c_kv = rms_norm(kv_a, kv_norm)
    k_pe = apply_rope(k_pe, cos[None], sin[None])
    w_kvb_h = w_kvb.reshape(KV_LORA_RANK, N_HEADS, QK_NOPE_DIM + V_HEAD_DIM)
    w_uk, w_uv = w_kvb_h[..., :QK_NOPE_DIM], w_kvb_h[..., QK_NOPE_DIM:]
    q_lat = jnp.einsum('bthn,chn->bthc', q_nope, w_uk)
    if cache_lat is None:
        o_lat = prompt_attention(q_lat, q_pe, c_kv, k_pe)
    else:
        c_all = jnp.concatenate([cache_lat.astype(c_kv.dtype), c_kv], axis=1)
        kpe_all = jnp.concatenate([cache_kpe.astype(k_pe.dtype), k_pe], axis=1)
        o_lat = mla_attend(q_lat, q_pe, c_all, kpe_all, None)
    attn = jnp.einsum('bthc,chv->bthv', o_lat, w_uv).reshape(B, T, N_HEADS * V_HEAD_DIM) @ w_o_attn

    a, g = jnp.split(conv_in, 2, axis=-1)
    u = a * jax.nn.sigmoid(g)
    if conv_prev is None:
        conv_prev = jnp.zeros((B, CONV_WIDTH - 1, CONV_CH), x.dtype)
    u, conv_state = causal_dwconv(u, conv_prev, conv_dw, conv_dw_b)
    conv = jax.nn.silu(layer_norm(u, conv_ln_g, conv_ln_b)) @ w_conv_out

    g_attn, g_conv = jnp.split(jax.nn.sigmoid(gates), 2, axis=-1)
    x = x + (g_attn * attn + g_conv * conv) @ w_out

    up = rms_norm(x, norm_ffn) @ w_up
    if ffn_prev is None:
        ffn_prev = jnp.zeros((B, FFN_CONV_WIDTH - 1, 2 * D_FF), x.dtype)
    up, ffn_state = causal_dwconv(up, ffn_prev, ffn_dw, ffn_dw_b)
    ga, v = jnp.split(up, 2, axis=-1)
    x = x + (jax.nn.silu(ga) * v) @ w_down
    return x, c_kv, k_pe, conv_state, ffn_state


def setup_inputs(seed: int = 0) -> dict:
    key = jax.random.key(seed)
    ks = jax.random.split(key, 26)

    def nrm(k, shape, scale):
        return jax.random.normal(k, shape, jnp.float32) * scale

    def gain(k, shape):
        return 1.0 + 0.01 * jax.random.normal(k, shape, jnp.float32)

    L = DEPTH
    return {
        'x_prompt': nrm(ks[0], (BATCH, SEQ, D_MODEL), 1.0),
        'x_sample': nrm(ks[1], (DEC_BATCH, DEC_SEQ, D_MODEL), 1.0),
        'cache_kv_latent': nrm(ks[2], (L, DEC_BATCH, PAST_LEN, KV_LORA_RANK), 1.0),
        'cache_k_rope': nrm(ks[3], (L, DEC_BATCH, PAST_LEN, QK_ROPE_DIM), 1.0),
        'state_conv': nrm(ks[4], (L, DEC_BATCH, CONV_WIDTH - 1, CONV_CH), 0.5),
        'state_ffn_conv': nrm(ks[5], (L, DEC_BATCH, FFN_CONV_WIDTH - 1, 2 * D_FF), 1.0),
        'norm_mix': gain(ks[6], (L, D_MODEL)),
        'w_in': nrm(ks[7], (L, D_MODEL, IN_COLS), D_MODEL ** -0.5),
        'q_norm': gain(ks[8], (L, Q_LORA_RANK)),
        'w_qb': nrm(ks[9], (L, Q_LORA_RANK, N_HEADS * (QK_NOPE_DIM + QK_ROPE_DIM)), Q_LORA_RANK ** -0.5),
        'kv_norm': gain(ks[10], (L, KV_LORA_RANK)),
        'w_kvb': nrm(ks[11], (L, KV_LORA_RANK, N_HEADS * (QK_NOPE_DIM + V_HEAD_DIM)), KV_LORA_RANK ** -0.5),
        'w_o_attn': nrm(ks[12], (L, N_HEADS * V_HEAD_DIM, D_MODEL), (N_HEADS * V_HEAD_DIM) ** -0.5),
        'conv_dw': nrm(ks[13], (L, CONV_WIDTH, CONV_CH), CONV_WIDTH ** -0.5),
        'conv_dw_b': nrm(ks[14], (L, CONV_CH), 0.01),
        'conv_ln_g': gain(ks[15], (L, CONV_CH)),
        'conv_ln_b': nrm(ks[16], (L, CONV_CH), 0.01),
        'w_conv_out': nrm(ks[17], (L, CONV_CH, D_MODEL), CONV_CH ** -0.5),
        'w_out': nrm(ks[18], (L, D_MODEL, D_MODEL), D_MODEL ** -0.5),
        'norm_ffn': gain(ks[19], (L, D_MODEL)),
        'w_up': nrm(ks[20], (L, D_MODEL, 2 * D_FF), D_MODEL ** -0.5),
        'ffn_dw': nrm(ks[21], (L, FFN_CONV_WIDTH, 2 * D_FF), FFN_CONV_WIDTH ** -0.5),
        'ffn_dw_b': nrm(ks[22], (L, 2 * D_FF), 0.01),
        'w_down': nrm(ks[23], (L, D_FF, D_MODEL), D_FF ** -0.5),
        'norm_final': gain(ks[24], (D_MODEL,)),
    }


def reference(x_prompt, x_sample, cache_kv_latent, cache_k_rope, state_conv, state_ffn_conv,
              norm_mix, w_in, q_norm, w_qb, kv_norm, w_kvb, w_o_attn,
              conv_dw, conv_dw_b, conv_ln_g, conv_ln_b, w_conv_out, w_out,
              norm_ffn, w_up, ffn_dw, ffn_dw_b, w_down, norm_final):
    pos_p = jnp.arange(x_prompt.shape[1])
    pos_s = cache_kv_latent.shape[2] + jnp.arange(x_sample.shape[1])
    xp, xs = x_prompt, x_sample
    p_lat, p_kpe, p_conv, p_ffn = [], [], [], []
    s_lat, s_kpe, s_conv, s_ffn = [], [], [], []
    for l in range(DEPTH):
        lw = [norm_mix[l], w_in[l], q_norm[l], w_qb[l], kv_norm[l], w_kvb[l], w_o_attn[l],
              conv_dw[l], conv_dw_b[l], conv_ln_g[l], conv_ln_b[l], w_conv_out[l], w_out[l],
              norm_ffn[l], w_up[l], ffn_dw[l], ffn_dw_b[l], w_down[l]]
        xp, c1, k1, cs1, fs1 = hybrid_layer(xp, pos_p, None, None, None, None, *lw)
        xs, c2, k2, cs2, fs2 = hybrid_layer(xs, pos_s, cache_kv_latent[l], cache_k_rope[l],
                                            state_conv[l], state_ffn_conv[l], *lw)
        p_lat.append(c1); p_kpe.append(k1); p_conv.append(cs1); p_ffn.append(fs1)
        s_lat.append(c2); s_kpe.append(k2); s_conv.append(cs2); s_ffn.append(fs2)
    y_prompt = rms_norm(xp, norm_final)
    y_sample = rms_norm(xs, norm_final)
    return (y_prompt, y_sample,
            jnp.stack(p_lat), jnp.stack(p_kpe), jnp.stack(p_conv), jnp.stack(p_ffn),
            jnp.stack(s_lat), jnp.stack(s_kpe), jnp.stack(s_conv), jnp.stack(s_ffn))
```

```python
import functools

import jax
import jax.numpy as jnp
from jax import lax
from jax.experimental import pallas as pl
from jax.experimental.pallas import tpu as pltpu

N_HEADS = 8
QK_NOPE_DIM = 128
QK_ROPE_DIM = 64
V_HEAD_DIM = 128
Q_LORA_RANK = 384
KV_LORA_RANK = 256
CHUNK = 64
ROPE_THETA = 10000.0
ATTN_SCALE = (QK_NOPE_DIM + QK_ROPE_DIM) ** -0.5
CONV_WIDTH = 31
FFN_CONV_WIDTH = 3
EPS = 1e-6

LANES = 128
SUBLANES = 8
VMEM_LIMIT_BYTES = 56 * 1024 * 1024
CONV_HALO = 32
FFN_HALO = SUBLANES
FFN_COLS = 256

BF16 = jnp.bfloat16
F32 = jnp.float32


def _dot(a, b):
    return jnp.dot(a, b, preferred_element_type=F32)


def _dot_nt(a, b):
    return lax.dot_general(a, b, (((1,), (1,)), ((), ())), preferred_element_type=F32)


def _rms(x, g):
    return x * lax.rsqrt(jnp.mean(x * x, axis=-1, keepdims=True) + EPS) * g


def _const_spec(shape):
    nd = len(shape)
    return pl.BlockSpec(shape, lambda *_: (0,) * nd, pipeline_mode=pl.Buffered(1))


def _params(n_axes):
    return pltpu.CompilerParams(dimension_semantics=("arbitrary",) * n_axes,
                                vmem_limit_bytes=VMEM_LIMIT_BYTES)


def _inproj_kernel(x_ref, tab_ref, nm_ref, wqa_ref, wkva_ref, wkpe_ref, wa_ref, wg_ref, wgt_ref,
                   qn_ref, wqn_ref, wqr_ref, kvn_ref, wuk_ref,
                   qlat_ref, qpe_ref, ckv_ref, ckvb_ref, kpe_ref, kpeb_ref, u_ref, sg_ref):
    x = x_ref[0]
    xn = _rms(x, nm_ref[...]).astype(BF16)
    tab = tab_ref[...]

    def rope(pair):
        t = pair * tab
        return (t + pltpu.roll(t, QK_ROPE_DIM, 1))[:, :QK_ROPE_DIM]

    qn = _rms(_dot(xn, wqa_ref[...]), qn_ref[...]).astype(BF16)
    q_nope = _dot(qn, wqn_ref[...])
    q_rope = _dot(qn, wqr_ref[...])
    for h in range(N_HEADS):
        cols = slice(h * LANES, (h + 1) * LANES)
        q_lat = _dot(q_nope[:, cols].astype(BF16), wuk_ref[h]) * ATTN_SCALE
        qlat_ref[0, h] = q_lat.astype(BF16)
        qpe_ref[0, h] = (rope(q_rope[:, cols]) * ATTN_SCALE).astype(BF16)

    ckv = _rms(_dot(xn, wkva_ref[...]), kvn_ref[...])
    ckv_ref[0] = ckv
    ckvb_ref[0] = ckv.astype(BF16)
    kpe = rope(_dot(xn, wkpe_ref[...]))
    kpe_ref[0] = kpe
    kpeb_ref[0] = kpe.astype(BF16)

    u_ref[0] = _dot(xn, wa_ref[...]) * jax.nn.sigmoid(_dot(xn, wg_ref[...]))
    sg_ref[0] = jax.nn.sigmoid(_dot(xn, wgt_ref[...]))


def _inproj(x, tab, lw, tm):
    B, T, D = x.shape
    H = N_HEADS
    C = lw["wa"].shape[1]
    grid = (B, T // tm)
    row = lambda b, t: (b, t, 0)
    in_specs = [pl.BlockSpec((1, tm, D), row),
                pl.BlockSpec((tm, LANES), lambda b, t: (t, 0))]
    weights = [lw["norm_mix"], lw["wqa"], lw["wkva"], lw["wkpe"], lw["wa"], lw["wg"], lw["wgt"],
               lw["q_norm"], lw["wqn"], lw["wqr"], lw["kv_norm"], lw["wuk"]]
    in_specs += [_const_spec(w.shape) for w in weights]
    out_shape = [jax.ShapeDtypeStruct((B, H, T, KV_LORA_RANK), BF16),
                 jax.ShapeDtypeStruct((B, H, T, QK_ROPE_DIM), BF16),
                 jax.ShapeDtypeStruct((B, T, KV_LORA_RANK), F32),
                 jax.ShapeDtypeStruct((B, T, KV_LORA_RANK), BF16),
                 jax.ShapeDtypeStruct((B, T, QK_ROPE_DIM), F32),
                 jax.ShapeDtypeStruct((B, T, QK_ROPE_DIM), BF16),
                 jax.ShapeDtypeStruct((B, T, C), F32),
                 jax.ShapeDtypeStruct((B, T, 2 * D), F32)]
    head = lambda b, t: (b, 0, t, 0)
    out_specs = [pl.BlockSpec((1, H, tm, KV_LORA_RANK), head),
                 pl.BlockSpec((1, H, tm, QK_ROPE_DIM), head),
                 pl.BlockSpec((1, tm, KV_LORA_RANK), row),
                 pl.BlockSpec((1, tm, KV_LORA_RANK), row),
                 pl.BlockSpec((1, tm, QK_ROPE_DIM), row),
                 pl.BlockSpec((1, tm, QK_ROPE_DIM), row),
                 pl.BlockSpec((1, tm, C), row),
                 pl.BlockSpec((1, tm, 2 * D), row)]
    return pl.pallas_call(_inproj_kernel, out_shape=out_shape, grid=grid, in_specs=in_specs,
                          out_specs=out_specs, compiler_params=_params(2), name="inproj")(x, tab, *weights)


def _conv_kernel(u_ref, prev_ref, w_ref, b_ref, lg_ref, lb_ref, o_ref, ubuf, ybuf, *, tc, rb):
    t = pl.program_id(1)
    n_slabs = ubuf.shape[0]
    for c in range(n_slabs):
        cols = slice(c * LANES, (c + 1) * LANES)

        @pl.when(t == 0)
        def _():
            ubuf[c, 0:CONV_HALO, :] = prev_ref[0, :, cols]

        @pl.when(t > 0)
        def _():
            ubuf[c, 0:CONV_HALO, :] = ubuf[c, tc:tc + CONV_HALO, :]

        ubuf[c, CONV_HALO:CONV_HALO + tc, :] = u_ref[0, :, cols]

    first = CONV_HALO - (CONV_WIDTH - 1)
    for c in range(n_slabs):
        cols = slice(c * LANES, (c + 1) * LANES)
        taps = [jnp.broadcast_to(w_ref[k:k + 1, cols], (rb, LANES)) for k in range(CONV_WIDTH)]
        bias = jnp.broadcast_to(b_ref[:, cols], (rb, LANES))

        def conv_rows(r, carry):
            r0 = r * (2 * rb)
            for phase in range(2):
                acc = bias
                for k in range(CONV_WIDTH):
                    acc = acc + taps[k] * ubuf[c, pl.ds(r0 + phase + first + k, rb, stride=2), :]
                ybuf[c, pl.ds(r0 + phase, rb, stride=2), :] = acc
            return carry

        lax.fori_loop(0, tc // (2 * rb), conv_rows, 0)

    lg = lg_ref[...]
    lb = lb_ref[...]
    nb = min(tc, 64)

    def norm_rows(r, carry):
        r0 = pl.multiple_of(r * nb, nb)
        y = jnp.concatenate([ybuf[c, pl.ds(r0, nb), :] for c in range(n_slabs)], axis=-1)
        d = y - jnp.mean(y, axis=-1, keepdims=True)
        yn = d * lax.rsqrt(jnp.mean(d * d, axis=-1, keepdims=True) + EPS) * lg + lb
        o_ref[0, pl.ds(r0, nb), :] = (yn * jax.nn.sigmoid(yn)).astype(BF16)
        return carry

    lax.fori_loop(0, tc // nb, norm_rows, 0)


def _conv(u, prev, lw, tc):
    B, T, C = u.shape
    assert tc >= CONV_HALO or T == tc
    rb = min(64, tc // 2)
    n_slabs = C // LANES
    row = lambda b, t: (b, t, 0)
    weights = [lw["conv_dw"], lw["conv_dw_b"], lw["conv_ln_g"], lw["conv_ln_b"]]
    return pl.pallas_call(
        functools.partial(_conv_kernel, tc=tc, rb=rb),
        out_shape=jax.ShapeDtypeStruct((B, T, C), BF16),
        grid=(B, T // tc),
        in_specs=[pl.BlockSpec((1, tc, C), row),
                  pl.BlockSpec((1, CONV_HALO, C), lambda b, t: (b, 0, 0))]
                 + [_const_spec(w.shape) for w in weights],
        out_specs=pl.BlockSpec((1, tc, C), row),
        scratch_shapes=[pltpu.VMEM((n_slabs, tc + CONV_HALO, LANES), F32), pltpu.VMEM((n_slabs, tc, LANES), F32)],
        compiler_params=_params(2), name="dwconv")(u, prev, *weights)


def _heads_out(o, wuv_ref, o_ref, rows):
    for h in range(N_HEADS):
        oh = o[h * rows:(h + 1) * rows, :].astype(BF16)
        o_ref[0, :, h * V_HEAD_DIM:(h + 1) * V_HEAD_DIM] = _dot(oh, wuv_ref[h]).astype(BF16)


def _attn_kernel(ql_ref, qp_ref, ckv_ref, kpe_ref, wuv_ref, o_ref, m_sc, l_sc, acc_sc, *, tq):
    i = pl.program_id(1)
    M = N_HEADS * tq
    tk = 2 * tq
    ql = ql_ref[0].reshape(M, KV_LORA_RANK)
    qp = qp_ref[0].reshape(M, QK_ROPE_DIM)
    m_sc[...] = jnp.full(m_sc.shape, -jnp.inf, F32)
    l_sc[...] = jnp.zeros(l_sc.shape, F32)
    acc_sc[...] = jnp.zeros(acc_sc.shape, F32)

    def step(k0, size, diagonal):
        kc = ckv_ref[0, pl.ds(k0, size), :]
        kp = kpe_ref[0, pl.ds(k0, size), :]
        s = _dot_nt(ql, kc) + _dot_nt(qp, kp)
        if diagonal:
            qc = lax.broadcasted_iota(jnp.int32, (tq, size), 0) // CHUNK
            kch = lax.broadcasted_iota(jnp.int32, (tq, size), 1) // CHUNK
            s = jnp.where((kch <= qc)[None], s.reshape(N_HEADS, tq, size), -jnp.inf).reshape(M, size)
        m_prev = m_sc[...]
        m_new = jnp.maximum(m_prev, jnp.max(s, axis=-1, keepdims=True))
        alpha = jnp.exp(m_prev - m_new)
        p = jnp.exp(s - m_new)
        l_sc[...] = alpha * l_sc[...] + jnp.sum(p, axis=-1, keepdims=True)
        acc_sc[...] = alpha * acc_sc[...] + _dot(p.astype(BF16), kc)
        m_sc[...] = m_new

    def full_step(k, carry):
        step(pl.multiple_of(k * tk, tk), tk, False)
        return carry

    lax.fori_loop(0, i // 2, full_step, 0)

    @pl.when(i % 2 == 1)
    def _():
        step(pl.multiple_of((i - 1) * tq, tq), tq, False)

    step(pl.multiple_of(i * tq, tq), tq, True)
    _heads_out(acc_sc[...] / l_sc[...], wuv_ref, o_ref, tq)


def _attn(qlat, qpe, ckvb, kpeb, wuv, tq):
    B, H, T, _ = qlat.shape
    head = lambda b, i: (b, 0, i, 0)
    whole = lambda b, i: (b, 0, 0)
    M = H * tq
    return pl.pallas_call(
        functools.partial(_attn_kernel, tq=tq),
        out_shape=jax.ShapeDtypeStruct((B, T, H * V_HEAD_DIM), BF16),
        grid=(B, T // tq),
        in_specs=[pl.BlockSpec((1, H, tq, KV_LORA_RANK), head),
                  pl.BlockSpec((1, H, tq, QK_ROPE_DIM), head),
                  pl.BlockSpec((1, T, KV_LORA_RANK), whole),
                  pl.BlockSpec((1, T, QK_ROPE_DIM), whole),
                  _const_spec(wuv.shape)],
        out_specs=pl.BlockSpec((1, tq, H * V_HEAD_DIM), lambda b, i: (b, i, 0)),
        scratch_shapes=[pltpu.VMEM((M, 1), F32), pltpu.VMEM((M, 1), F32), pltpu.VMEM((M, KV_LORA_RANK), F32)],
        compiler_params=_params(2), name="attn_prompt")(qlat, qpe, ckvb, kpeb, wuv)


def _attn_cache_kernel(ql_ref, qp_ref, clat_ref, ckpe_ref, nkv_ref, nkpe_ref, wuv_ref, o_ref, kall, pall,
                       *, ts, past):
    M = N_HEADS * ts
    n_keys = kall.shape[0]
    ql = ql_ref[0].reshape(M, KV_LORA_RANK)
    qp = qp_ref[0].reshape(M, QK_ROPE_DIM)
    kall[0:past, :] = clat_ref[0].astype(BF16)
    pall[0:past, :] = ckpe_ref[0].astype(BF16)
    kall[past:n_keys, :] = jnp.zeros((n_keys - past, KV_LORA_RANK), BF16)
    pall[past:n_keys, :] = jnp.zeros((n_keys - past, QK_ROPE_DIM), BF16)
    kall[past:past + ts, :] = nkv_ref[0]
    pall[past:past + ts, :] = nkpe_ref[0]
    kc = kall[...]
    s = _dot_nt(ql, kc) + _dot_nt(qp, pall[...])
    valid = lax.broadcasted_iota(jnp.int32, s.shape, 1) < past + ts
    s = jnp.where(valid, s, -jnp.inf)
    p = jnp.exp(s - jnp.max(s, axis=-1, keepdims=True))
    o = _dot(p.astype(BF16), kc) / jnp.sum(p, axis=-1, keepdims=True)
    _heads_out(o, wuv_ref, o_ref, ts)


def _attn_cache(qlat, qpe, cache_lat, cache_kpe, nkv, nkpe, wuv):
    B, H, ts, _ = qlat.shape
    past = cache_lat.shape[1]
    n_keys = past + LANES
    assert ts <= LANES
    b4 = lambda b: (b, 0, 0, 0)
    b3 = lambda b: (b, 0, 0)
    return pl.pallas_call(
        functools.partial(_attn_cache_kernel, ts=ts, past=past),
        out_shape=jax.ShapeDtypeStruct((B, ts, H * V_HEAD_DIM), BF16),
        grid=(B,),
        in_specs=[pl.BlockSpec((1, H, ts, KV_LORA_RANK), b4),
                  pl.BlockSpec((1, H, ts, QK_ROPE_DIM), b4),
                  pl.BlockSpec((1, past, KV_LORA_RANK), b3),
                  pl.BlockSpec((1, past, QK_ROPE_DIM), b3),
                  pl.BlockSpec((1, ts, KV_LORA_RANK), b3),
                  pl.BlockSpec((1, ts, QK_ROPE_DIM), b3),
                  _const_spec(wuv.shape)],
        out_specs=pl.BlockSpec((1, ts, H * V_HEAD_DIM), b3),
        scratch_shapes=[pltpu.VMEM((n_keys, KV_LORA_RANK), BF16), pltpu.VMEM((n_keys, QK_ROPE_DIM), BF16)],
        compiler_params=_params(1), name="attn_cache")(qlat, qpe, cache_lat, cache_kpe, nkv, nkpe, wuv)


def _outffn_kernel(*refs, tm, final):
    (x_ref, ap_ref, ca_ref, sg_ref, prev_ref, woa_ref, wco_ref, wout_ref, nf_ref, wup_ref, fw_ref, fb_ref,
     wdn_ref) = refs[:13]
    if final:
        nfin_ref, xo_ref, st_ref, y_ref, upbuf = refs[13:]
    else:
        xo_ref, st_ref, upbuf = refs[13:]
    t = pl.program_id(1)
    D = x_ref.shape[-1]
    d_ff = wdn_ref.shape[0]

    sg = sg_ref[0]
    merged = sg[:, :D] * _dot(ap_ref[0], woa_ref[...]) + sg[:, D:] * _dot(ca_ref[0], wco_ref[...])
    x1 = x_ref[0] + _dot(merged.astype(BF16), wout_ref[...])
    xn = _rms(x1, nf_ref[...]).astype(BF16)

    @pl.when(t == 0)
    def _():
        upbuf[0:FFN_HALO, :] = prev_ref[0]

    @pl.when(t > 0)
    def _():
        upbuf[0:FFN_HALO, :] = upbuf[tm:tm + FFN_HALO, :]

    def conv3(cols):
        upbuf[FFN_HALO:FFN_HALO + tm, cols] = _dot(xn, wup_ref[:, cols])
        y = fb_ref[:, cols]
        for k in range(FFN_CONV_WIDTH):
            lo = FFN_HALO - (FFN_CONV_WIDTH - 1) + k
            y = y + fw_ref[k:k + 1, cols] * upbuf[lo:lo + tm, cols]
        return y

    acc = jnp.zeros((tm, D), F32)
    for c in range(d_ff // FFN_COLS):
        ga = conv3(slice(c * FFN_COLS, (c + 1) * FFN_COLS))
        v = conv3(slice(d_ff + c * FFN_COLS, d_ff + (c + 1) * FFN_COLS))
        act = (ga * jax.nn.sigmoid(ga) * v).astype(BF16)
        acc = acc + _dot(act, wdn_ref[c * FFN_COLS:(c + 1) * FFN_COLS, :])
    x2 = x1 + acc
    xo_ref[0] = x2
    st_ref[0] = upbuf[tm:tm + FFN_HALO, :]
    if final:
        y_ref[0] = _rms(x2, nfin_ref[...])


def _outffn(x, ap, ca, sg, prev, lw, norm_final, tm):
    B, T, D = x.shape
    d_up = lw["wup"].shape[1]
    final = norm_final is not None
    row = lambda b, t: (b, t, 0)
    per_batch = lambda b, t: (b, 0, 0)
    weights = [lw["woa"], lw["wco"], lw["wout"], lw["norm_ffn"], lw["wup"], lw["ffn_dw"], lw["ffn_dw_b"], lw["wdn"]]
    if final:
        weights.append(norm_final)
    in_specs = [pl.BlockSpec((1, tm, D), row), pl.BlockSpec((1, tm, D), row), pl.BlockSpec((1, tm, D), row),
                pl.BlockSpec((1, tm, 2 * D), row), pl.BlockSpec((1, FFN_HALO, d_up), per_batch)]
    in_specs += [_const_spec(w.shape) for w in weights]
    out_shape = [jax.ShapeDtypeStruct((B, T, D), F32), jax.ShapeDtypeStruct((B, FFN_HALO, d_up), F32)]
    out_specs = [pl.BlockSpec((1, tm, D), row), pl.BlockSpec((1, FFN_HALO, d_up), per_batch)]
    if final:
        out_shape.append(jax.ShapeDtypeStruct((B, T, D), F32))
        out_specs.append(pl.BlockSpec((1, tm, D), row))
    return pl.pallas_call(
        functools.partial(_outffn_kernel, tm=tm, final=final),
        out_shape=out_shape, grid=(B, T // tm), in_specs=in_specs, out_specs=out_specs,
        scratch_shapes=[pltpu.VMEM((tm + FFN_HALO, d_up), F32)],
        compiler_params=_params(2), name="outffn_final" if final else "outffn")(x, ap, ca, sg, prev, *weights)


def _swap_halves(w):
    half = w.shape[-1] // 2
    return jnp.concatenate([w[..., half:], w[..., :half]], axis=-1)


def _layer_weights(norm_mix, w_in, q_norm, w_qb, kv_norm, w_kvb, w_o_attn, conv_dw, conv_dw_b, conv_ln_g,
                   conv_ln_b, w_conv_out, w_out, norm_ffn, w_up, ffn_dw, ffn_dw_b, w_down):
    D = w_in.shape[0]
    C = conv_dw.shape[1]
    H = N_HEADS
    o0 = Q_LORA_RANK
    o1 = o0 + KV_LORA_RANK
    o2 = o1 + QK_ROPE_DIM
    o3 = o2 + C
    o4 = o3 + C
    w_kpe = w_in[:, o1:o2]
    w_qb_h = w_qb.reshape(Q_LORA_RANK, H, QK_NOPE_DIM + QK_ROPE_DIM)
    w_q_rope = w_qb_h[..., QK_NOPE_DIM:]
    w_kvb_h = w_kvb.reshape(KV_LORA_RANK, H, QK_NOPE_DIM + V_HEAD_DIM)
    row = lambda v: v.reshape(1, -1)
    return dict(
        norm_mix=row(norm_mix), q_norm=row(q_norm), kv_norm=row(kv_norm), norm_ffn=row(norm_ffn),
        wqa=w_in[:, :o0].astype(BF16), wkva=w_in[:, o0:o1].astype(BF16),
        wkpe=jnp.concatenate([w_kpe, _swap_halves(w_kpe)], axis=-1).astype(BF16),
        wa=w_in[:, o2:o3].astype(BF16), wg=w_in[:, o3:o4].astype(BF16), wgt=w_in[:, o4:].astype(BF16),
        wqn=w_qb_h[..., :QK_NOPE_DIM].reshape(Q_LORA_RANK, H * QK_NOPE_DIM).astype(BF16),
        wqr=jnp.concatenate([w_q_rope, _swap_halves(w_q_rope)], axis=-1).reshape(Q_LORA_RANK, H * LANES).astype(BF16),
        wuk=jnp.transpose(w_kvb_h[..., :QK_NOPE_DIM], (1, 2, 0)).astype(BF16),
        wuv=jnp.transpose(w_kvb_h[..., QK_NOPE_DIM:], (1, 0, 2)).astype(BF16),
        woa=w_o_attn.astype(BF16), wco=w_conv_out.astype(BF16), wout=w_out.astype(BF16),
        conv_dw=conv_dw, conv_dw_b=row(conv_dw_b), conv_ln_g=row(conv_ln_g), conv_ln_b=row(conv_ln_b),
        wup=w_up.astype(BF16), ffn_dw=ffn_dw, ffn_dw_b=row(ffn_dw_b), wdn=w_down.astype(BF16))


def _rope_table(pos):
    half = QK_ROPE_DIM // 2
    inv = jnp.power(ROPE_THETA, -jnp.arange(half, dtype=F32) / half)
    ang = pos.astype(F32)[:, None] * inv[None, :]
    cos, sin = jnp.cos(ang), jnp.sin(ang)
    return jnp.concatenate([cos, cos, -sin, sin], axis=-1)


def _pad_front(state, rows):
    B, n, C = state.shape
    return jnp.concatenate([jnp.zeros((B, rows - n, C), state.dtype), state], axis=1)


def _tile(T, want):
    t = min(T, want)
    assert T % t == 0
    return t


def _layer(x, tab, cache, conv_prev, ffn_prev, lw, norm_final):
    B, T, D = x.shape
    qlat, qpe, ckv, ckvb, kpe, kpeb, u, sg = _inproj(x, tab, lw, _tile(T, 256))
    if cache is None:
        ap = _attn(qlat, qpe, ckvb, kpeb, lw["wuv"], _tile(T, 256))
    else:
        ap = _attn_cache(qlat, qpe, cache[0], cache[1], ckvb, kpeb, lw["wuv"])
    ca = _conv(u, _pad_front(conv_prev, CONV_HALO), lw, _tile(T, 512))
    outs = _outffn(x, ap, ca, sg, _pad_front(ffn_prev, FFN_HALO), lw, norm_final, _tile(T, 256))
    conv_state = jnp.concatenate([conv_prev, u], axis=1)[:, -(CONV_WIDTH - 1):]
    ffn_state = outs[1][:, -(FFN_CONV_WIDTH - 1):]
    y = outs[2] if norm_final is not None else None
    return outs[0], y, ckv, kpe, conv_state, ffn_state


def kernel(x_prompt, x_sample, cache_kv_latent, cache_k_rope, state_conv, state_ffn_conv, norm_mix, w_in, q_norm,
           w_qb, kv_norm, w_kvb, w_o_attn, conv_dw, conv_dw_b, conv_ln_g, conv_ln_b, w_conv_out, w_out, norm_ffn,
           w_up, ffn_dw, ffn_dw_b, w_down, norm_final):
    depth = w_in.shape[0]
    Bp, Tp, D = x_prompt.shape
    Bs, Ts, _ = x_sample.shape
    past = cache_kv_latent.shape[2]
    tab_p = _rope_table(jnp.arange(Tp))
    tab_s = _rope_table(past + jnp.arange(Ts))
    per_layer = [norm_mix, w_in, q_norm, w_qb, kv_norm, w_kvb, w_o_attn, conv_dw, conv_dw_b, conv_ln_g, conv_ln_b,
                 w_conv_out, w_out, norm_ffn, w_up, ffn_dw, ffn_dw_b, w_down]
    nfin = norm_final.reshape(1, -1)
    xp, xs = x_prompt, x_sample
    outs_p, outs_s = [], []
    yp = ys = None
    for l in range(depth):
        lw = _layer_weights(*[w[l] for w in per_layer])
        last = nfin if l == depth - 1 else None
        zc = jnp.zeros((Bp, CONV_WIDTH - 1, conv_dw.shape[2]), F32)
        zf = jnp.zeros((Bp, FFN_CONV_WIDTH - 1, w_up.shape[2]), F32)
        xp, yp, *sp = _layer(xp, tab_p, None, zc, zf, lw, last)
        xs, ys, *ss = _layer(xs, tab_s, (cache_kv_latent[l], cache_k_rope[l]), state_conv[l], state_ffn_conv[l],
                             lw, last)
        outs_p.append(sp)
        outs_s.append(ss)
    stack = lambda outs, j: jnp.stack([o[j] for o in outs])
    return (yp, ys,
            stack(outs_p, 0), stack(outs_p, 1), stack(outs_p, 2), stack(outs_p, 3),
            stack(outs_s, 0), stack(outs_s, 1), stack(outs_s, 2), stack(outs_s, 3))
```

```python
import functools
import math

import jax
import jax.numpy as jnp
from jax import lax
from jax.experimental import pallas as pl
from jax.experimental.pallas import tpu as pltpu

N_HEADS = 8
QK_NOPE_DIM = 128
QK_ROPE_DIM = 64
QK_DIM = QK_NOPE_DIM + QK_ROPE_DIM
V_HEAD_DIM = 128
Q_LORA_RANK = 384
KV_LORA_RANK = 256
CHUNK = 64
ROPE_THETA = 10000.0
ATTN_SCALE = QK_DIM ** -0.5
Q_SCALE = ATTN_SCALE * math.log2(math.e)
CONV_WIDTH = 31
FFN_CONV_WIDTH = 3
EPS = 1e-6

LANES = 128
SUBLANES = 8
VMEM_LIMIT_BYTES = 56 * 1024 * 1024
CONV_HALO = 32
FFN_HALO = SUBLANES
FFN_COLS = 256
ATTN_ROWS = 256
ATTN_QSTEP = 2048

BF16 = jnp.bfloat16
F32 = jnp.float32


def _dot(a, b):
    return jnp.dot(a, b, preferred_element_type=F32)


def _dot_nt(a, b):
    return lax.dot_general(a, b, (((1,), (1,)), ((), ())), preferred_element_type=F32)


def _rms(x, g):
    return x * lax.rsqrt(jnp.mean(x * x, axis=-1, keepdims=True) + EPS) * g


def _const_spec(shape):
    nd = len(shape)
    return pl.BlockSpec(shape, lambda *_: (0,) * nd, pipeline_mode=pl.Buffered(1))


def _params(n_axes):
    return pltpu.CompilerParams(dimension_semantics=("arbitrary",) * n_axes,
                                vmem_limit_bytes=VMEM_LIMIT_BYTES)


def _inproj_kernel(*refs, latent):
    (x_ref, tab_ref, nm_ref, wqa_ref, wkva_ref, wkpe_ref, wa_ref, wg_ref, wgt_ref, qn_ref, wqn_ref, wqr_ref,
     kvn_ref) = refs[:13]
    if latent:
        wuk_ref, qlat_ref, qpe_ref, ckv_ref, ckvb_ref, kpe_ref, kpeb_ref, u_ref, sg_ref = refs[13:]
    else:
        wukt_ref, wuv_ref, q_ref, kt_ref, v_ref, ckv_ref, kpe_ref, u_ref, sg_ref = refs[13:]
    x = x_ref[0]
    xn = _rms(x, nm_ref[...]).astype(BF16)
    tab = tab_ref[...]

    def rope(pair):
        t = pair * tab
        return t + pltpu.roll(t, QK_ROPE_DIM, 1)

    qn = _rms(_dot(xn, wqa_ref[...]), qn_ref[...]).astype(BF16)
    q_nope = _dot(qn, wqn_ref[...])
    q_rope = _dot(qn, wqr_ref[...])
    ckv = _rms(_dot(xn, wkva_ref[...]), kvn_ref[...])
    kpe = rope(_dot(xn, wkpe_ref[...]))
    ckv_ref[0] = ckv
    kpe_ref[0] = kpe[:, :QK_ROPE_DIM]
    ckvb = ckv.astype(BF16)

    if latent:
        ckvb_ref[0] = ckvb
        kpeb_ref[0] = kpe[:, :QK_ROPE_DIM].astype(BF16)
        for h in range(N_HEADS):
            cols = slice(h * LANES, (h + 1) * LANES)
            q_lat = _dot(q_nope[:, cols].astype(BF16), wuk_ref[h]) * Q_SCALE
            qlat_ref[0, h] = q_lat.astype(BF16)
            qpe_ref[0, h] = (rope(q_rope[:, cols])[:, :QK_ROPE_DIM] * Q_SCALE).astype(BF16)
    else:
        knt = _dot(wukt_ref[...], ckv.T.astype(BF16))
        kpet = kpe.T[:QK_ROPE_DIM, :].astype(BF16)
        v_all = _dot(ckvb, wuv_ref[...])
        for h in range(N_HEADS):
            cols = slice(h * LANES, (h + 1) * LANES)
            kt_ref[0, h, 0, 0:QK_NOPE_DIM, :] = knt[h * QK_NOPE_DIM:(h + 1) * QK_NOPE_DIM, :].astype(BF16)
            kt_ref[0, h, 0, QK_NOPE_DIM:QK_DIM, :] = kpet
            v_ref[0, h] = v_all[:, cols].astype(BF16)
            q_ref[0, h, :, 0:QK_NOPE_DIM] = (q_nope[:, cols] * Q_SCALE).astype(BF16)
            q_ref[0, h, :, QK_NOPE_DIM:QK_DIM] = (rope(q_rope[:, cols])[:, :QK_ROPE_DIM] * Q_SCALE).astype(BF16)

    u_ref[0] = _dot(xn, wa_ref[...]) * jax.nn.sigmoid(_dot(xn, wg_ref[...]))
    sg_ref[0] = jax.nn.sigmoid(_dot(xn, wgt_ref[...]))


def _inproj(x, tab, lw, tm, latent):
    B, T, D = x.shape
    H = N_HEADS
    C = lw["wa"].shape[1]
    row = lambda b, t: (b, t, 0)
    head = lambda b, t: (b, 0, t, 0)
    in_specs = [pl.BlockSpec((1, tm, D), row),
                pl.BlockSpec((tm, LANES), lambda b, t: (t, 0))]
    weights = [lw["norm_mix"], lw["wqa"], lw["wkva"], lw["wkpe"], lw["wa"], lw["wg"], lw["wgt"],
               lw["q_norm"], lw["wqn"], lw["wqr"], lw["kv_norm"]]
    weights += [lw["wuk"]] if latent else [lw["wukt"], lw["wuv_all"]]
    in_specs += [_const_spec(w.shape) for w in weights]
    f32_rows = lambda n: (jax.ShapeDtypeStruct((B, T, n), F32), pl.BlockSpec((1, tm, n), row))
    bf16_rows = lambda n: (jax.ShapeDtypeStruct((B, T, n), BF16), pl.BlockSpec((1, tm, n), row))
    bf16_heads = lambda n: (jax.ShapeDtypeStruct((B, H, T, n), BF16), pl.BlockSpec((1, H, tm, n), head))
    if latent:
        outs = [bf16_heads(KV_LORA_RANK), bf16_heads(QK_ROPE_DIM), f32_rows(KV_LORA_RANK), bf16_rows(KV_LORA_RANK),
                f32_rows(QK_ROPE_DIM), bf16_rows(QK_ROPE_DIM)]
    else:
        keys_t = (jax.ShapeDtypeStruct((B, H, T // tm, QK_DIM, tm), BF16),
                  pl.BlockSpec((1, H, 1, QK_DIM, tm), lambda b, t: (b, 0, t, 0, 0)))
        outs = [bf16_heads(QK_DIM), keys_t, bf16_heads(V_HEAD_DIM), f32_rows(KV_LORA_RANK), f32_rows(QK_ROPE_DIM)]
    outs += [f32_rows(C), f32_rows(2 * D)]
    return pl.pallas_call(functools.partial(_inproj_kernel, latent=latent),
                          out_shape=[o[0] for o in outs], grid=(B, T // tm), in_specs=in_specs,
                          out_specs=[o[1] for o in outs], compiler_params=_params(2),
                          name="inproj_latent" if latent else "inproj")(x, tab, *weights)


def _conv_kernel(u_ref, prev_ref, w_ref, b_ref, lg_ref, lb_ref, o_ref, ubuf, ybuf, *, tc, rb):
    t = pl.program_id(1)
    n_slabs = ubuf.shape[0]
    for c in range(n_slabs):
        cols = slice(c * LANES, (c + 1) * LANES)

        @pl.when(t == 0)
        def _():
            ubuf[c, 0:CONV_HALO, :] = prev_ref[0, :, cols]

        @pl.when(t > 0)
        def _():
            ubuf[c, 0:CONV_HALO, :] = ubuf[c, tc:tc + CONV_HALO, :]

        ubuf[c, CONV_HALO:CONV_HALO + tc, :] = u_ref[0, :, cols]

    first = CONV_HALO - (CONV_WIDTH - 1)
    for c in range(n_slabs):
        cols = slice(c * LANES, (c + 1) * LANES)
        taps = [jnp.broadcast_to(w_ref[k:k + 1, cols], (rb, LANES)) for k in range(CONV_WIDTH)]
        bias = jnp.broadcast_to(b_ref[:, cols], (rb, LANES))

        def conv_rows(r, carry):
            r0 = r * (2 * rb)
            for phase in range(2):
                acc = bias
                for k in range(CONV_WIDTH):
                    acc = acc + taps[k] * ubuf[c, pl.ds(r0 + phase + first + k, rb, stride=2), :]
                ybuf[c, pl.ds(r0 + phase, rb, stride=2), :] = acc
            return carry

        lax.fori_loop(0, tc // (2 * rb), conv_rows, 0)

    lg = lg_ref[...]
    lb = lb_ref[...]
    nb = min(tc, 64)

    def norm_rows(r, carry):
        r0 = pl.multiple_of(r * nb, nb)
        y = jnp.concatenate([ybuf[c, pl.ds(r0, nb), :] for c in range(n_slabs)], axis=-1)
        d = y - jnp.mean(y, axis=-1, keepdims=True)
        yn = d * lax.rsqrt(jnp.mean(d * d, axis=-1, keepdims=True) + EPS) * lg + lb
        o_ref[0, pl.ds(r0, nb), :] = (yn * jax.nn.sigmoid(yn)).astype(BF16)
        return carry

    lax.fori_loop(0, tc // nb, norm_rows, 0)


def _conv(u, prev, lw, tc):
    B, T, C = u.shape
    assert tc >= CONV_HALO or T == tc
    rb = min(64, tc // 2)
    n_slabs = C // LANES
    row = lambda b, t: (b, t, 0)
    weights = [lw["conv_dw"], lw["conv_dw_b"], lw["conv_ln_g"], lw["conv_ln_b"]]
    return pl.pallas_call(
        functools.partial(_conv_kernel, tc=tc, rb=rb),
        out_shape=jax.ShapeDtypeStruct((B, T, C), BF16),
        grid=(B, T // tc),
        in_specs=[pl.BlockSpec((1, tc, C), row),
                  pl.BlockSpec((1, CONV_HALO, C), lambda b, t: (b, 0, 0))]
                 + [_const_spec(w.shape) for w in weights],
        out_specs=pl.BlockSpec((1, tc, C), row),
        scratch_shapes=[pltpu.VMEM((n_slabs, tc + CONV_HALO, LANES), F32), pltpu.VMEM((n_slabs, tc, LANES), F32)],
        compiler_params=_params(2), name="dwconv")(u, prev, *weights)


def _attn_kernel(q_ref, kt_ref, v_ref, o_ref, m_sc, acc_sc, *, tq, rg, kd):
    T = q_ref.shape[2]
    G = tq // rg
    pieces_per_step = tq // kd

    def ones_col(rows):
        return jnp.where(lax.broadcasted_iota(jnp.int32, (rows, LANES), 1) == 0, 1.0, 0.0).astype(BF16)

    def update(g, q0, key_pieces, v, keep):
        q = q_ref[0, 0, pl.ds(q0 + g * rg, rg), :]
        s = jnp.concatenate([_dot(q, kt) for kt in key_pieces], axis=1)
        width = s.shape[1]
        if keep is not None:
            s = jnp.where(keep, s, -jnp.inf)
        m_prev = m_sc[g]
        m_new = jnp.maximum(m_prev, jnp.max(s, axis=-1, keepdims=True))
        alpha = jnp.exp2(m_prev - m_new)
        p = jnp.exp2(s - jnp.tile(m_new, (1, width // LANES))).astype(BF16)
        pv = _dot(p, jnp.concatenate([v, ones_col(width)], axis=1))
        acc_sc[g] = jnp.tile(alpha, (1, 2)) * acc_sc[g] + pv
        m_sc[g] = m_new

    def qstep(i, carry):
        q0 = pl.multiple_of(i * tq, tq)
        m_sc[...] = jnp.full(m_sc.shape, -jnp.inf, F32)
        acc_sc[...] = jnp.zeros(acc_sc.shape, F32)

        def unmasked_block(k, c):
            key_pieces = [kt_ref[0, 0, k * pieces_per_step + j] for j in range(pieces_per_step)]
            v = v_ref[0, 0, pl.ds(pl.multiple_of(k * tq, tq), tq), :]
            for g in range(G):
                update(g, q0, key_pieces, v, None)
            return c

        lax.fori_loop(0, i, unmasked_block, 0)

        for g in range(G):
            width = (g + 1) * rg
            key_pieces = [kt_ref[0, 0, i * pieces_per_step + j] for j in range(width // kd)]
            if width % kd:
                key_pieces.append(kt_ref[0, 0, i * pieces_per_step + width // kd][:, :width % kd])
            v = v_ref[0, 0, pl.ds(q0, width), :]
            q_chunk = (lax.broadcasted_iota(jnp.int32, (rg, width), 0) + g * rg) // CHUNK
            k_chunk = lax.broadcasted_iota(jnp.int32, (rg, width), 1) // CHUNK
            update(g, q0, key_pieces, v, k_chunk <= q_chunk)

        for g in range(G):
            a = acc_sc[g]
            o_ref[0, pl.ds(q0 + g * rg, rg), :] = (a[:, :V_HEAD_DIM] / a[:, V_HEAD_DIM:V_HEAD_DIM + 1]).astype(BF16)
        return carry

    lax.fori_loop(0, T // tq, qstep, 0)


def _attn(q, kt, v):
    B, H, T, _ = q.shape
    kd = kt.shape[-1]
    tq = min(T, ATTN_QSTEP)
    rg = min(tq, ATTN_ROWS)
    assert T % tq == 0 and tq % rg == 0 and tq % kd == 0 and rg % CHUNK == 0 and (rg % kd == 0 or kd % rg == 0)
    G = tq // rg
    return pl.pallas_call(
        functools.partial(_attn_kernel, tq=tq, rg=rg, kd=kd),
        out_shape=jax.ShapeDtypeStruct((B, T, H * V_HEAD_DIM), BF16),
        grid=(B, H),
        in_specs=[pl.BlockSpec((1, 1, T, QK_DIM), lambda b, h: (b, h, 0, 0)),
                  pl.BlockSpec((1, 1, T // kd, QK_DIM, kd), lambda b, h: (b, h, 0, 0, 0)),
                  pl.BlockSpec((1, 1, T, V_HEAD_DIM), lambda b, h: (b, h, 0, 0))],
        out_specs=pl.BlockSpec((1, T, V_HEAD_DIM), lambda b, h: (b, 0, h)),
        scratch_shapes=[pltpu.VMEM((G, rg, LANES), F32), pltpu.VMEM((G, rg, 2 * LANES), F32)],
        compiler_params=_params(2), name="attn_prompt")(q, kt, v)


def _attn_cache_kernel(ql_ref, qp_ref, clat_ref, ckpe_ref, nkv_ref, nkpe_ref, wuv_ref, o_ref, kall, pall,
                       *, ts, past):
    M = N_HEADS * ts
    n_keys = kall.shape[0]
    ql = ql_ref[0].reshape(M, KV_LORA_RANK)
    qp = qp_ref[0].reshape(M, QK_ROPE_DIM)
    kall[0:past, :] = clat_ref[0].astype(BF16)
    pall[0:past, :] = ckpe_ref[0].astype(BF16)
    kall[past:n_keys, :] = jnp.zeros((n_keys - past, KV_LORA_RANK), BF16)
    pall[past:n_keys, :] = jnp.zeros((n_keys - past, QK_ROPE_DIM), BF16)
    kall[past:past + ts, :] = nkv_ref[0]
    pall[past:past + ts, :] = nkpe_ref[0]
    kc = kall[...]
    s = _dot_nt(ql, kc) + _dot_nt(qp, pall[...])
    valid = lax.broadcasted_iota(jnp.int32, s.shape, 1) < past + ts
    s = jnp.where(valid, s, -jnp.inf)
    p = jnp.exp2(s - jnp.max(s, axis=-1, keepdims=True))
    o = _dot(p.astype(BF16), kc) / jnp.sum(p, axis=-1, keepdims=True)
    for h in range(N_HEADS):
        oh = o[h * ts:(h + 1) * ts, :].astype(BF16)
        o_ref[0, :, h * V_HEAD_DIM:(h + 1) * V_HEAD_DIM] = _dot(oh, wuv_ref[h]).astype(BF16)


def _attn_cache(qlat, qpe, cache_lat, cache_kpe, nkv, nkpe, wuv):
    B, H, ts, _ = qlat.shape
    past = cache_lat.shape[1]
    n_keys = past + LANES
    assert ts <= LANES
    b4 = lambda b: (b, 0, 0, 0)
    b3 = lambda b: (b, 0, 0)
    return pl.pallas_call(
        functools.partial(_attn_cache_kernel, ts=ts, past=past),
        out_shape=jax.ShapeDtypeStruct((B, ts, H * V_HEAD_DIM), BF16),
        grid=(B,),
        in_specs=[pl.BlockSpec((1, H, ts, KV_LORA_RANK), b4),
                  pl.BlockSpec((1, H, ts, QK_ROPE_DIM), b4),
                  pl.BlockSpec((1, past, KV_LORA_RANK), b3),
                  pl.BlockSpec((1, past, QK_ROPE_DIM), b3),
                  pl.BlockSpec((1, ts, KV_LORA_RANK), b3),
                  pl.BlockSpec((1, ts, QK_ROPE_DIM), b3),
                  _const_spec(wuv.shape)],
        out_specs=pl.BlockSpec((1, ts, H * V_HEAD_DIM), b3),
        scratch_shapes=[pltpu.VMEM((n_keys, KV_LORA_RANK), BF16), pltpu.VMEM((n_keys, QK_ROPE_DIM), BF16)],
        compiler_params=_params(1), name="attn_cache")(qlat, qpe, cache_lat, cache_kpe, nkv, nkpe, wuv)


def _outffn_kernel(*refs, tm, final):
    (x_ref, ap_ref, ca_ref, sg_ref, prev_ref, woa_ref, wco_ref, wout_ref, nf_ref, wup_ref, fw_ref, fb_ref,
     wdn_ref) = refs[:13]
    if final:
        nfin_ref, xo_ref, st_ref, y_ref, upbuf = refs[13:]
    else:
        xo_ref, st_ref, upbuf = refs[13:]
    t = pl.program_id(1)
    D = x_ref.shape[-1]
    d_ff = wdn_ref.shape[0]

    sg = sg_ref[0]
    merged = sg[:, :D] * _dot(ap_ref[0], woa_ref[...]) + sg[:, D:] * _dot(ca_ref[0], wco_ref[...])
    x1 = x_ref[0] + _dot(merged.astype(BF16), wout_ref[...])
    xn = _rms(x1, nf_ref[...]).astype(BF16)

    @pl.when(t == 0)
    def _():
        upbuf[0:FFN_HALO, :] = prev_ref[0]

    @pl.when(t > 0)
    def _():
        upbuf[0:FFN_HALO, :] = upbuf[tm:tm + FFN_HALO, :]

    def conv3(cols):
        upbuf[FFN_HALO:FFN_HALO + tm, cols] = _dot(xn, wup_ref[:, cols])
        y = fb_ref[:, cols]
        for k in range(FFN_CONV_WIDTH):
            lo = FFN_HALO - (FFN_CONV_WIDTH - 1) + k
            y = y + fw_ref[k:k + 1, cols] * upbuf[lo:lo + tm, cols]
        return y

    acc = jnp.zeros((tm, D), F32)
    for c in range(d_ff // FFN_COLS):
        ga = conv3(slice(c * FFN_COLS, (c + 1) * FFN_COLS))
        v = conv3(slice(d_ff + c * FFN_COLS, d_ff + (c + 1) * FFN_COLS))
        act = (ga * jax.nn.sigmoid(ga) * v).astype(BF16)
        acc = acc + _dot(act, wdn_ref[c * FFN_COLS:(c + 1) * FFN_COLS, :])
    x2 = x1 + acc
    xo_ref[0] = x2
    st_ref[0] = upbuf[tm:tm + FFN_HALO, :]
    if final:
        y_ref[0] = _rms(x2, nfin_ref[...])


def _outffn(x, ap, ca, sg, prev, lw, norm_final, tm):
    B, T, D = x.shape
    d_up = lw["wup"].shape[1]
    final = norm_final is not None
    row = lambda b, t: (b, t, 0)
    per_batch = lambda b, t: (b, 0, 0)
    weights = [lw["woa"], lw["wco"], lw["wout"], lw["norm_ffn"], lw["wup"], lw["ffn_dw"], lw["ffn_dw_b"], lw["wdn"]]
    if final:
        weights.append(norm_final)
    in_specs = [pl.BlockSpec((1, tm, D), row), pl.BlockSpec((1, tm, D), row), pl.BlockSpec((1, tm, D), row),
                pl.BlockSpec((1, tm, 2 * D), row), pl.BlockSpec((1, FFN_HALO, d_up), per_batch)]
    in_specs += [_const_spec(w.shape) for w in weights]
    out_shape = [jax.ShapeDtypeStruct((B, T, D), F32), jax.ShapeDtypeStruct((B, FFN_HALO, d_up), F32)]
    out_specs = [pl.BlockSpec((1, tm, D), row), pl.BlockSpec((1, FFN_HALO, d_up), per_batch)]
    if final:
        out_shape.append(jax.ShapeDtypeStruct((B, T, D), F32))
        out_specs.append(pl.BlockSpec((1, tm, D), row))
    return pl.pallas_call(
        functools.partial(_outffn_kernel, tm=tm, final=final),
        out_shape=out_shape, grid=(B, T // tm), in_specs=in_specs, out_specs=out_specs,
        scratch_shapes=[pltpu.VMEM((tm + FFN_HALO, d_up), F32)],
        compiler_params=_params(2), name="outffn_final" if final else "outffn")(x, ap, ca, sg, prev, *weights)


def _swap_halves(w):
    half = w.shape[-1] // 2
    return jnp.concatenate([w[..., half:], w[..., :half]], axis=-1)


def _layer_weights(norm_mix, w_in, q_norm, w_qb, kv_norm, w_kvb, w_o_attn, conv_dw, conv_dw_b, conv_ln_g,
                   conv_ln_b, w_conv_out, w_out, norm_ffn, w_up, ffn_dw, ffn_dw_b, w_down):
    C = conv_dw.shape[1]
    H = N_HEADS
    o0 = Q_LORA_RANK
    o1 = o0 + KV_LORA_RANK
    o2 = o1 + QK_ROPE_DIM
    o3 = o2 + C
    o4 = o3 + C
    w_kpe = w_in[:, o1:o2]
    w_qb_h = w_qb.reshape(Q_LORA_RANK, H, QK_DIM)
    w_q_rope = w_qb_h[..., QK_NOPE_DIM:]
    w_kvb_h = w_kvb.reshape(KV_LORA_RANK, H, QK_NOPE_DIM + V_HEAD_DIM)
    w_uk = w_kvb_h[..., :QK_NOPE_DIM]
    w_uv = w_kvb_h[..., QK_NOPE_DIM:]
    row = lambda v: v.reshape(1, -1)
    return dict(
        norm_mix=row(norm_mix), q_norm=row(q_norm), kv_norm=row(kv_norm), norm_ffn=row(norm_ffn),
        wqa=w_in[:, :o0].astype(BF16), wkva=w_in[:, o0:o1].astype(BF16),
        wkpe=jnp.concatenate([w_kpe, _swap_halves(w_kpe)], axis=-1).astype(BF16),
        wa=w_in[:, o2:o3].astype(BF16), wg=w_in[:, o3:o4].astype(BF16), wgt=w_in[:, o4:].astype(BF16),
        wqn=w_qb_h[..., :QK_NOPE_DIM].reshape(Q_LORA_RANK, H * QK_NOPE_DIM).astype(BF16),
        wqr=jnp.concatenate([w_q_rope, _swap_halves(w_q_rope)], axis=-1).reshape(Q_LORA_RANK, H * LANES).astype(BF16),
        wuk=jnp.transpose(w_uk, (1, 2, 0)).astype(BF16),
        wuv=jnp.transpose(w_uv, (1, 0, 2)).astype(BF16),
        wukt=jnp.transpose(w_uk, (1, 2, 0)).reshape(H * QK_NOPE_DIM, KV_LORA_RANK).astype(BF16),
        wuv_all=w_uv.reshape(KV_LORA_RANK, H * V_HEAD_DIM).astype(BF16),
        woa=w_o_attn.astype(BF16), wco=w_conv_out.astype(BF16), wout=w_out.astype(BF16),
        conv_dw=conv_dw, conv_dw_b=row(conv_dw_b), conv_ln_g=row(conv_ln_g), conv_ln_b=row(conv_ln_b),
        wup=w_up.astype(BF16), ffn_dw=ffn_dw, ffn_dw_b=row(ffn_dw_b), wdn=w_down.astype(BF16))


def _rope_table(pos):
    half = QK_ROPE_DIM // 2
    inv = jnp.power(ROPE_THETA, -jnp.arange(half, dtype=F32) / half)
    ang = pos.astype(F32)[:, None] * inv[None, :]
    cos, sin = jnp.cos(ang), jnp.sin(ang)
    return jnp.concatenate([cos, cos, -sin, sin], axis=-1)


def _pad_front(state, rows):
    B, n, C = state.shape
    return jnp.concatenate([jnp.zeros((B, rows - n, C), state.dtype), state], axis=1)


def _tile(T, want):
    t = min(T, want)
    assert T % t == 0
    return t


def _layer(x, tab, cache, conv_prev, ffn_prev, lw, norm_final):
    B, T, D = x.shape
    if cache is None:
        q, kt, v, ckv, kpe, u, sg = _inproj(x, tab, lw, _tile(T, 512), latent=False)
        ap = _attn(q, kt, v)
    else:
        qlat, qpe, ckv, ckvb, kpe, kpeb, u, sg = _inproj(x, tab, lw, _tile(T, 512), latent=True)
        ap = _attn_cache(qlat, qpe, cache[0], cache[1], ckvb, kpeb, lw["wuv"])
    ca = _conv(u, _pad_front(conv_prev, CONV_HALO), lw, _tile(T, 512))
    outs = _outffn(x, ap, ca, sg, _pad_front(ffn_prev, FFN_HALO), lw, norm_final, _tile(T, 256))
    conv_state = jnp.concatenate([conv_prev, u], axis=1)[:, -(CONV_WIDTH - 1):]
    ffn_state = outs[1][:, -(FFN_CONV_WIDTH - 1):]
    y = outs[2] if norm_final is not None else None
    return outs[0], y, ckv, kpe, conv_state, ffn_state


def kernel(x_prompt, x_sample, cache_kv_latent, cache_k_rope, state_conv, state_ffn_conv, norm_mix, w_in, q_norm,
           w_qb, kv_norm, w_kvb, w_o_attn, conv_dw, conv_dw_b, conv_ln_g, conv_ln_b, w_conv_out, w_out, norm_ffn,
           w_up, ffn_dw, ffn_dw_b, w_down, norm_final):
    depth = w_in.shape[0]
    Bp, Tp, D = x_prompt.shape
    Bs, Ts, _ = x_sample.shape
    past = cache_kv_latent.shape[2]
    tab_p = _rope_table(jnp.arange(Tp))
    tab_s = _rope_table(past + jnp.arange(Ts))
    per_layer = [norm_mix, w_in, q_norm, w_qb, kv_norm, w_kvb, w_o_attn, conv_dw, conv_dw_b, conv_ln_g, conv_ln_b,
                 w_conv_out, w_out, norm_ffn, w_up, ffn_dw, ffn_dw_b, w_down]
    nfin = norm_final.reshape(1, -1)
    xp, xs = x_prompt, x_sample
    outs_p, outs_s = [], []
    yp = ys = None
    for l in range(depth):
        lw = _layer_weights(*[w[l] for w in per_layer])
        last = nfin if l == depth - 1 else None
        zc = jnp.zeros((Bp, CONV_WIDTH - 1, conv_dw.shape[2]), F32)
        zf = jnp.zeros((Bp, FFN_CONV_WIDTH - 1, w_up.shape[2]), F32)
        xp, yp, *sp = _layer(xp, tab_p, None, zc, zf, lw, last)
        xs, ys, *ss = _layer(xs, tab_s, (cache_kv_latent[l], cache_k_rope[l]), state_conv[l], state_ffn_conv[l],
                             lw, last)
        outs_p.append(sp)
        outs_s.append(ss)
    stack = lambda outs, j: jnp.stack([o[j] for o in outs])
    return (yp, ys,
            stack(outs_p, 0), stack(outs_p, 1), stack(outs_p, 2), stack(outs_p, 3),
            stack(outs_s, 0), stack(outs_s, 1), stack(outs_s, 2), stack(outs_s, 3))
```

```python
import functools
import math

import jax
import jax.numpy as jnp
from jax import lax
from jax.experimental import pallas as pl
from jax.experimental.pallas import tpu as pltpu

N_HEADS = 8
QK_NOPE_DIM = 128
QK_ROPE_DIM = 64
QK_DIM = QK_NOPE_DIM + QK_ROPE_DIM
V_HEAD_DIM = 128
Q_LORA_RANK = 384
KV_LORA_RANK = 256
CHUNK = 64
ROPE_THETA = 10000.0
ATTN_SCALE = QK_DIM ** -0.5
Q_SCALE = ATTN_SCALE * math.log2(math.e)
CONV_WIDTH = 31
FFN_CONV_WIDTH = 3
EPS = 1e-6

LANES = 128
SUBLANES = 8
VMEM_LIMIT_BYTES = 56 * 1024 * 1024
CONV_HALO = 32
FFN_HALO = SUBLANES
FFN_COLS = 256
ATTN_ROWS = 256
ATTN_QSTEP = 2048

BF16 = jnp.bfloat16
F32 = jnp.float32


def _dot(a, b):
    return jnp.dot(a, b, preferred_element_type=F32)


def _dot_nt(a, b):
    return lax.dot_general(a, b, (((1,), (1,)), ((), ())), preferred_element_type=F32)


def _rms(x, g):
    return x * lax.rsqrt(jnp.mean(x * x, axis=-1, keepdims=True) + EPS) * g


def _const_spec(shape):
    nd = len(shape)
    return pl.BlockSpec(shape, lambda *_: (0,) * nd, pipeline_mode=pl.Buffered(1))


def _params(n_axes):
    return pltpu.CompilerParams(dimension_semantics=("arbitrary",) * n_axes,
                                vmem_limit_bytes=VMEM_LIMIT_BYTES)


def _inproj_kernel(*refs, latent):
    (x_ref, tab_ref, nm_ref, wqa_ref, wkva_ref, wkpe_ref, wa_ref, wg_ref, wgt_ref, qn_ref, wqn_ref, wqr_ref,
     kvn_ref) = refs[:13]
    if latent:
        wuk_ref, qlat_ref, qpe_ref, ckv_ref, ckvb_ref, kpe_ref, kpeb_ref, u_ref, sg_ref = refs[13:]
    else:
        wukt_ref, wuv_ref, q_ref, kt_ref, v_ref, ckv_ref, kpe_ref, u_ref, sg_ref = refs[13:]
    x = x_ref[0]
    xn = _rms(x, nm_ref[...]).astype(BF16)
    tab = tab_ref[...]

    def rope(pair):
        t = pair * tab
        return t + pltpu.roll(t, QK_ROPE_DIM, 1)

    qn = _rms(_dot(xn, wqa_ref[...]), qn_ref[...]).astype(BF16)
    q_nope = _dot(qn, wqn_ref[...])
    q_rope = _dot(qn, wqr_ref[...])
    ckv = _rms(_dot(xn, wkva_ref[...]), kvn_ref[...])
    kpe = rope(_dot(xn, wkpe_ref[...]))
    ckv_ref[0] = ckv
    kpe_ref[0] = kpe[:, :QK_ROPE_DIM]
    ckvb = ckv.astype(BF16)

    if latent:
        ckvb_ref[0] = ckvb
        kpeb_ref[0] = kpe[:, :QK_ROPE_DIM].astype(BF16)
        for h in range(N_HEADS):
            cols = slice(h * LANES, (h + 1) * LANES)
            q_lat = _dot(q_nope[:, cols].astype(BF16), wuk_ref[h]) * Q_SCALE
            qlat_ref[0, h] = q_lat.astype(BF16)
            qpe_ref[0, h] = (rope(q_rope[:, cols])[:, :QK_ROPE_DIM] * Q_SCALE).astype(BF16)
    else:
        knt = _dot(wukt_ref[...], ckv.T.astype(BF16))
        kpet = kpe.T[:QK_ROPE_DIM, :].astype(BF16)
        v_all = _dot(ckvb, wuv_ref[...])
        for h in range(N_HEADS):
            cols = slice(h * LANES, (h + 1) * LANES)
            kt_ref[0, h, 0, 0:QK_NOPE_DIM, :] = knt[h * QK_NOPE_DIM:(h + 1) * QK_NOPE_DIM, :].astype(BF16)
            kt_ref[0, h, 0, QK_NOPE_DIM:QK_DIM, :] = kpet
            v_ref[0, h] = v_all[:, cols].astype(BF16)
            q_ref[0, h, :, 0:QK_NOPE_DIM] = (q_nope[:, cols] * Q_SCALE).astype(BF16)
            q_ref[0, h, :, QK_NOPE_DIM:QK_DIM] = (rope(q_rope[:, cols])[:, :QK_ROPE_DIM] * Q_SCALE).astype(BF16)

    u_ref[0] = _dot(xn, wa_ref[...]) * jax.nn.sigmoid(_dot(xn, wg_ref[...]))
    sg_ref[0] = jax.nn.sigmoid(_dot(xn, wgt_ref[...]))


def _inproj(x, tab, lw, tm, latent):
    B, T, D = x.shape
    H = N_HEADS
    C = lw["wa"].shape[1]
    row = lambda b, t: (b, t, 0)
    head = lambda b, t: (b, 0, t, 0)
    in_specs = [pl.BlockSpec((1, tm, D), row),
                pl.BlockSpec((tm, LANES), lambda b, t: (t, 0))]
    weights = [lw["norm_mix"], lw["wqa"], lw["wkva"], lw["wkpe"], lw["wa"], lw["wg"], lw["wgt"],
               lw["q_norm"], lw["wqn"], lw["wqr"], lw["kv_norm"]]
    weights += [lw["wuk"]] if latent else [lw["wukt"], lw["wuv_all"]]
    in_specs += [_const_spec(w.shape) for w in weights]
    f32_rows = lambda n: (jax.ShapeDtypeStruct((B, T, n), F32), pl.BlockSpec((1, tm, n), row))
    bf16_rows = lambda n: (jax.ShapeDtypeStruct((B, T, n), BF16), pl.BlockSpec((1, tm, n), row))
    bf16_heads = lambda n: (jax.ShapeDtypeStruct((B, H, T, n), BF16), pl.BlockSpec((1, H, tm, n), head))
    if latent:
        outs = [bf16_heads(KV_LORA_RANK), bf16_heads(QK_ROPE_DIM), f32_rows(KV_LORA_RANK), bf16_rows(KV_LORA_RANK),
                f32_rows(QK_ROPE_DIM), bf16_rows(QK_ROPE_DIM)]
    else:
        keys_t = (jax.ShapeDtypeStruct((B, H, T // tm, QK_DIM, tm), BF16),
                  pl.BlockSpec((1, H, 1, QK_DIM, tm), lambda b, t: (b, 0, t, 0, 0)))
        outs = [bf16_heads(QK_DIM), keys_t, bf16_heads(V_HEAD_DIM), f32_rows(KV_LORA_RANK), f32_rows(QK_ROPE_DIM)]
    outs += [f32_rows(C), f32_rows(2 * D)]
    return pl.pallas_call(functools.partial(_inproj_kernel, latent=latent),
                          out_shape=[o[0] for o in outs], grid=(B, T // tm), in_specs=in_specs,
                          out_specs=[o[1] for o in outs], compiler_params=_params(2),
                          name="inproj_latent" if latent else "inproj")(x, tab, *weights)


def _conv_kernel(u_ref, prev_ref, w_ref, b_ref, lg_ref, lb_ref, o_ref, ubuf, ybuf, *, tc, rb):
    t = pl.program_id(1)
    n_slabs = ubuf.shape[0]
    for c in range(n_slabs):
        cols = slice(c * LANES, (c + 1) * LANES)

        @pl.when(t == 0)
        def _():
            ubuf[c, 0:CONV_HALO, :] = prev_ref[0, :, cols]

        @pl.when(t > 0)
        def _():
            ubuf[c, 0:CONV_HALO, :] = ubuf[c, tc:tc + CONV_HALO, :]

        ubuf[c, CONV_HALO:CONV_HALO + tc, :] = u_ref[0, :, cols]

    first = CONV_HALO - (CONV_WIDTH - 1)
    for c in range(n_slabs):
        cols = slice(c * LANES, (c + 1) * LANES)
        taps = [jnp.broadcast_to(w_ref[k:k + 1, cols], (rb, LANES)) for k in range(CONV_WIDTH)]
        bias = jnp.broadcast_to(b_ref[:, cols], (rb, LANES))

        def conv_rows(r, carry):
            r0 = r * (2 * rb)
            for phase in range(2):
                acc = bias
                for k in range(CONV_WIDTH):
                    acc = acc + taps[k] * ubuf[c, pl.ds(r0 + phase + first + k, rb, stride=2), :]
                ybuf[c, pl.ds(r0 + phase, rb, stride=2), :] = acc
            return carry

        lax.fori_loop(0, tc // (2 * rb), conv_rows, 0)

    lg = lg_ref[...]
    lb = lb_ref[...]
    nb = min(tc, 128)

    def norm_rows(r, carry):
        r0 = pl.multiple_of(r * nb, nb)
        y = jnp.concatenate([ybuf[c, pl.ds(r0, nb), :] for c in range(n_slabs)], axis=-1)
        d = y - jnp.mean(y, axis=-1, keepdims=True)
        yn = d * lax.rsqrt(jnp.mean(d * d, axis=-1, keepdims=True) + EPS) * lg + lb
        o_ref[0, pl.ds(r0, nb), :] = (yn * jax.nn.sigmoid(yn)).astype(BF16)
        return carry

    lax.fori_loop(0, tc // nb, norm_rows, 0)


def _conv(u, prev, lw, tc):
    B, T, C = u.shape
    assert tc >= CONV_HALO or T == tc
    rb = min(64, tc // 2)
    n_slabs = C // LANES
    row = lambda b, t: (b, t, 0)
    weights = [lw["conv_dw"], lw["conv_dw_b"], lw["conv_ln_g"], lw["conv_ln_b"]]
    return pl.pallas_call(
        functools.partial(_conv_kernel, tc=tc, rb=rb),
        out_shape=jax.ShapeDtypeStruct((B, T, C), BF16),
        grid=(B, T // tc),
        in_specs=[pl.BlockSpec((1, tc, C), row),
                  pl.BlockSpec((1, CONV_HALO, C), lambda b, t: (b, 0, 0))]
                 + [_const_spec(w.shape) for w in weights],
        out_specs=pl.BlockSpec((1, tc, C), row),
        scratch_shapes=[pltpu.VMEM((n_slabs, tc + CONV_HALO, LANES), F32), pltpu.VMEM((n_slabs, tc, LANES), F32)],
        compiler_params=_params(2), name="dwconv")(u, prev, *weights)


def _attn_kernel(q_ref, kt_ref, v_ref, o_ref, m_sc, acc_sc, *, tq, rg, kd):
    T = q_ref.shape[2]
    G = tq // rg
    pieces_per_step = tq // kd

    def ones_col(rows):
        return jnp.where(lax.broadcasted_iota(jnp.int32, (rows, LANES), 1) == 0, 1.0, 0.0).astype(BF16)

    q_chunk = lax.broadcasted_iota(jnp.int32, (rg, rg), 0) // CHUNK
    k_chunk = lax.broadcasted_iota(jnp.int32, (rg, rg), 1) // CHUNK
    own_mask = jnp.where(k_chunk <= q_chunk, 0.0, -jnp.inf).astype(F32)

    def scores(g, q0, key_pieces):
        q = q_ref[0, 0, pl.ds(q0 + g * rg, rg), :]
        return jnp.concatenate([_dot(q, kt) for kt in key_pieces], axis=1)

    def update(g, s, v):
        width = s.shape[1]
        m_prev = m_sc[g]
        m_new = jnp.maximum(m_prev, jnp.max(s, axis=-1, keepdims=True))
        alpha = jnp.exp2(m_prev - m_new)
        p = jnp.exp2(s - jnp.tile(m_new, (1, width // LANES))).astype(BF16)
        pv = _dot(p, jnp.concatenate([v, ones_col(width)], axis=1))
        acc_sc[g] = jnp.tile(alpha, (1, 2)) * acc_sc[g] + pv
        m_sc[g] = m_new

    def run_groups(score_fn, value_fn):
        s_next = score_fn(0)
        for g in range(G):
            s = s_next
            if g + 1 < G:
                s_next = score_fn(g + 1)
            update(g, s, value_fn(g))

    def qstep(i, carry):
        q0 = pl.multiple_of(i * tq, tq)
        m_sc[...] = jnp.full(m_sc.shape, -jnp.inf, F32)
        acc_sc[...] = jnp.zeros(acc_sc.shape, F32)

        def unmasked_block(k, c):
            key_pieces = [kt_ref[0, 0, k * pieces_per_step + j] for j in range(pieces_per_step)]
            v = v_ref[0, 0, pl.ds(pl.multiple_of(k * tq, tq), tq), :]
            run_groups(lambda g: scores(g, q0, key_pieces), lambda g: v)
            return c

        lax.fori_loop(0, i, unmasked_block, 0)

        def own_scores(g):
            width = (g + 1) * rg
            key_pieces = [kt_ref[0, 0, i * pieces_per_step + j] for j in range(width // kd)]
            if width % kd:
                key_pieces.append(kt_ref[0, 0, i * pieces_per_step + width // kd][:, :width % kd])
            s = scores(g, q0, key_pieces)
            own = s[:, width - rg:] + own_mask
            return jnp.concatenate([s[:, :width - rg], own], axis=1) if g else own

        run_groups(own_scores, lambda g: v_ref[0, 0, pl.ds(q0, (g + 1) * rg), :])

        for g in range(G):
            a = acc_sc[g]
            o_ref[0, pl.ds(q0 + g * rg, rg), :] = (a[:, :V_HEAD_DIM] / a[:, V_HEAD_DIM:V_HEAD_DIM + 1]).astype(BF16)
        return carry

    lax.fori_loop(0, T // tq, qstep, 0)


def _attn(q, kt, v):
    B, H, T, _ = q.shape
    kd = kt.shape[-1]
    tq = min(T, ATTN_QSTEP)
    rg = min(tq, ATTN_ROWS)
    assert T % tq == 0 and tq % rg == 0 and tq % kd == 0 and rg % CHUNK == 0 and (rg % kd == 0 or kd % rg == 0)
    G = tq // rg
    return pl.pallas_call(
        functools.partial(_attn_kernel, tq=tq, rg=rg, kd=kd),
        out_shape=jax.ShapeDtypeStruct((B, T, H * V_HEAD_DIM), BF16),
        grid=(B, H),
        in_specs=[pl.BlockSpec((1, 1, T, QK_DIM), lambda b, h: (b, h, 0, 0)),
                  pl.BlockSpec((1, 1, T // kd, QK_DIM, kd), lambda b, h: (b, h, 0, 0, 0)),
                  pl.BlockSpec((1, 1, T, V_HEAD_DIM), lambda b, h: (b, h, 0, 0))],
        out_specs=pl.BlockSpec((1, T, V_HEAD_DIM), lambda b, h: (b, 0, h)),
        scratch_shapes=[pltpu.VMEM((G, rg, LANES), F32), pltpu.VMEM((G, rg, 2 * LANES), F32)],
        compiler_params=_params(2), name="attn_prompt")(q, kt, v)


def _attn_cache_kernel(ql_ref, qp_ref, clat_ref, ckpe_ref, nkv_ref, nkpe_ref, wuv_ref, o_ref, kall, pall,
                       *, ts, past):
    M = N_HEADS * ts
    n_keys = kall.shape[0]
    ql = ql_ref[0].reshape(M, KV_LORA_RANK)
    qp = qp_ref[0].reshape(M, QK_ROPE_DIM)
    kall[0:past, :] = clat_ref[0].astype(BF16)
    pall[0:past, :] = ckpe_ref[0].astype(BF16)
    kall[past:n_keys, :] = jnp.zeros((n_keys - past, KV_LORA_RANK), BF16)
    pall[past:n_keys, :] = jnp.zeros((n_keys - past, QK_ROPE_DIM), BF16)
    kall[past:past + ts, :] = nkv_ref[0]
    pall[past:past + ts, :] = nkpe_ref[0]
    kc = kall[...]
    s = _dot_nt(ql, kc) + _dot_nt(qp, pall[...])
    valid = lax.broadcasted_iota(jnp.int32, s.shape, 1) < past + ts
    s = jnp.where(valid, s, -jnp.inf)
    p = jnp.exp2(s - jnp.max(s, axis=-1, keepdims=True))
    o = _dot(p.astype(BF16), kc) / jnp.sum(p, axis=-1, keepdims=True)
    for h in range(N_HEADS):
        oh = o[h * ts:(h + 1) * ts, :].astype(BF16)
        o_ref[0, :, h * V_HEAD_DIM:(h + 1) * V_HEAD_DIM] = _dot(oh, wuv_ref[h]).astype(BF16)


def _attn_cache(qlat, qpe, cache_lat, cache_kpe, nkv, nkpe, wuv):
    B, H, ts, _ = qlat.shape
    past = cache_lat.shape[1]
    n_keys = past + LANES
    assert ts <= LANES
    b4 = lambda b: (b, 0, 0, 0)
    b3 = lambda b: (b, 0, 0)
    return pl.pallas_call(
        functools.partial(_attn_cache_kernel, ts=ts, past=past),
        out_shape=jax.ShapeDtypeStruct((B, ts, H * V_HEAD_DIM), BF16),
        grid=(B,),
        in_specs=[pl.BlockSpec((1, H, ts, KV_LORA_RANK), b4),
                  pl.BlockSpec((1, H, ts, QK_ROPE_DIM), b4),
                  pl.BlockSpec((1, past, KV_LORA_RANK), b3),
                  pl.BlockSpec((1, past, QK_ROPE_DIM), b3),
                  pl.BlockSpec((1, ts, KV_LORA_RANK), b3),
                  pl.BlockSpec((1, ts, QK_ROPE_DIM), b3),
                  _const_spec(wuv.shape)],
        out_specs=pl.BlockSpec((1, ts, H * V_HEAD_DIM), b3),
        scratch_shapes=[pltpu.VMEM((n_keys, KV_LORA_RANK), BF16), pltpu.VMEM((n_keys, QK_ROPE_DIM), BF16)],
        compiler_params=_params(1), name="attn_cache")(qlat, qpe, cache_lat, cache_kpe, nkv, nkpe, wuv)


def _outffn_kernel(*refs, tm, final):
    (x_ref, ap_ref, ca_ref, sg_ref, prev_ref, woa_ref, wco_ref, wout_ref, nf_ref, wup_ref, fw_ref, fb_ref,
     wdn_ref) = refs[:13]
    if final:
        nfin_ref, xo_ref, st_ref, y_ref, upbuf, carry, accbuf = refs[13:]
    else:
        xo_ref, st_ref, upbuf, carry, accbuf = refs[13:]
    t = pl.program_id(1)
    D = x_ref.shape[-1]
    d_ff = wdn_ref.shape[0]
    n_steps = d_ff // FFN_COLS
    slabs = FFN_COLS // LANES
    half = tm // 2

    sg = sg_ref[0]
    merged = sg[:, :D] * _dot(ap_ref[0], woa_ref[...]) + sg[:, D:] * _dot(ca_ref[0], wco_ref[...])
    x1 = x_ref[0] + _dot(merged.astype(BF16), wout_ref[...])
    xn = _rms(x1, nf_ref[...]).astype(BF16)

    @pl.when(t == 0)
    def _():
        carry[...] = prev_ref[0]

    def up_project(c):
        for part, base in enumerate((c * FFN_COLS, d_ff + c * FFN_COLS)):
            up = _dot(xn, wup_ref[:, base:base + FFN_COLS])
            for sl in range(slabs):
                cols = slice(base + sl * LANES, base + (sl + 1) * LANES)
                idx = ((c % 2) * 2 + part) * slabs + sl
                upbuf[idx, 0:FFN_HALO, :] = carry[:, cols]
                upbuf[idx, FFN_HALO:FFN_HALO + tm, :] = up[:, sl * LANES:(sl + 1) * LANES]
                carry[:, cols] = upbuf[idx, tm:tm + FFN_HALO, :]

    def conv3(c, part):
        base = (c * FFN_COLS, d_ff + c * FFN_COLS)[part]
        out = []
        for sl in range(slabs):
            cols = slice(base + sl * LANES, base + (sl + 1) * LANES)
            idx = ((c % 2) * 2 + part) * slabs + sl
            halves = []
            for phase in range(2):
                y = jnp.broadcast_to(fb_ref[:, cols], (half, LANES))
                for k in range(FFN_CONV_WIDTH):
                    first = FFN_HALO - (FFN_CONV_WIDTH - 1) + k + phase
                    y = y + fw_ref[k:k + 1, cols] * upbuf[idx, pl.ds(first, half, stride=2), :]
                halves.append(y)
            out.append(jnp.concatenate(halves, axis=0))
        return jnp.concatenate(out, axis=1)

    acc = jnp.zeros((tm, D), F32)
    up_project(0)
    for c in range(n_steps):
        if c + 1 < n_steps:
            up_project(c + 1)
        ga = conv3(c, 0)
        v = conv3(c, 1)
        act = (ga * jax.nn.sigmoid(ga) * v).astype(BF16)
        acc = acc + _dot(act, wdn_ref[c * FFN_COLS:(c + 1) * FFN_COLS, :])

    for sl in range(D // LANES):
        cols = slice(sl * LANES, (sl + 1) * LANES)
        accbuf[sl, pl.ds(0, half, stride=2), :] = acc[:half, cols]
        accbuf[sl, pl.ds(1, half, stride=2), :] = acc[half:, cols]
    x2 = x1 + jnp.concatenate([accbuf[sl] for sl in range(D // LANES)], axis=1)
    xo_ref[0] = x2
    st_ref[0] = carry[...]
    if final:
        y_ref[0] = _rms(x2, nfin_ref[...])


def _outffn(x, ap, ca, sg, prev, lw, norm_final, tm):
    B, T, D = x.shape
    d_up = lw["wup"].shape[1]
    final = norm_final is not None
    row = lambda b, t: (b, t, 0)
    per_batch = lambda b, t: (b, 0, 0)
    weights = [lw["woa"], lw["wco"], lw["wout"], lw["norm_ffn"], lw["wup"], lw["ffn_dw"], lw["ffn_dw_b"], lw["wdn"]]
    if final:
        weights.append(norm_final)
    in_specs = [pl.BlockSpec((1, tm, D), row), pl.BlockSpec((1, tm, D), row), pl.BlockSpec((1, tm, D), row),
                pl.BlockSpec((1, tm, 2 * D), row), pl.BlockSpec((1, FFN_HALO, d_up), per_batch)]
    in_specs += [_const_spec(w.shape) for w in weights]
    out_shape = [jax.ShapeDtypeStruct((B, T, D), F32), jax.ShapeDtypeStruct((B, FFN_HALO, d_up), F32)]
    out_specs = [pl.BlockSpec((1, tm, D), row), pl.BlockSpec((1, FFN_HALO, d_up), per_batch)]
    if final:
        out_shape.append(jax.ShapeDtypeStruct((B, T, D), F32))
        out_specs.append(pl.BlockSpec((1, tm, D), row))
    return pl.pallas_call(
        functools.partial(_outffn_kernel, tm=tm, final=final),
        out_shape=out_shape, grid=(B, T // tm), in_specs=in_specs, out_specs=out_specs,
        scratch_shapes=[pltpu.VMEM((4 * FFN_COLS // LANES, tm + FFN_HALO, LANES), F32),
                        pltpu.VMEM((FFN_HALO, d_up), F32),
                        pltpu.VMEM((D // LANES, tm, LANES), F32)],
        compiler_params=_params(2), name="outffn_final" if final else "outffn")(x, ap, ca, sg, prev, *weights)


def _swap_halves(w):
    half = w.shape[-1] // 2
    return jnp.concatenate([w[..., half:], w[..., :half]], axis=-1)


def _layer_weights(norm_mix, w_in, q_norm, w_qb, kv_norm, w_kvb, w_o_attn, conv_dw, conv_dw_b, conv_ln_g,
                   conv_ln_b, w_conv_out, w_out, norm_ffn, w_up, ffn_dw, ffn_dw_b, w_down):
    C = conv_dw.shape[1]
    H = N_HEADS
    o0 = Q_LORA_RANK
    o1 = o0 + KV_LORA_RANK
    o2 = o1 + QK_ROPE_DIM
    o3 = o2 + C
    o4 = o3 + C
    w_kpe = w_in[:, o1:o2]
    w_qb_h = w_qb.reshape(Q_LORA_RANK, H, QK_DIM)
    w_q_rope = w_qb_h[..., QK_NOPE_DIM:]
    w_kvb_h = w_kvb.reshape(KV_LORA_RANK, H, QK_NOPE_DIM + V_HEAD_DIM)
    w_uk = w_kvb_h[..., :QK_NOPE_DIM]
    w_uv = w_kvb_h[..., QK_NOPE_DIM:]
    row = lambda v: v.reshape(1, -1)
    return dict(
        norm_mix=row(norm_mix), q_norm=row(q_norm), kv_norm=row(kv_norm), norm_ffn=row(norm_ffn),
        wqa=w_in[:, :o0].astype(BF16), wkva=w_in[:, o0:o1].astype(BF16),
        wkpe=jnp.concatenate([w_kpe, _swap_halves(w_kpe)], axis=-1).astype(BF16),
        wa=w_in[:, o2:o3].astype(BF16), wg=w_in[:, o3:o4].astype(BF16), wgt=w_in[:, o4:].astype(BF16),
        wqn=w_qb_h[..., :QK_NOPE_DIM].reshape(Q_LORA_RANK, H * QK_NOPE_DIM).astype(BF16),
        wqr=jnp.concatenate([w_q_rope, _swap_halves(w_q_rope)], axis=-1).reshape(Q_LORA_RANK, H * LANES).astype(BF16),
        wuk=jnp.transpose(w_uk, (1, 2, 0)).astype(BF16),
        wuv=jnp.transpose(w_uv, (1, 0, 2)).astype(BF16),
        wukt=jnp.transpose(w_uk, (1, 2, 0)).reshape(H * QK_NOPE_DIM, KV_LORA_RANK).astype(BF16),
        wuv_all=w_uv.reshape(KV_LORA_RANK, H * V_HEAD_DIM).astype(BF16),
        woa=w_o_attn.astype(BF16), wco=w_conv_out.astype(BF16), wout=w_out.astype(BF16),
        conv_dw=conv_dw, conv_dw_b=row(conv_dw_b), conv_ln_g=row(conv_ln_g), conv_ln_b=row(conv_ln_b),
        wup=w_up.astype(BF16), ffn_dw=ffn_dw, ffn_dw_b=row(ffn_dw_b), wdn=w_down.astype(BF16))


def _rope_table(pos):
    half = QK_ROPE_DIM // 2
    inv = jnp.power(ROPE_THETA, -jnp.arange(half, dtype=F32) / half)
    ang = pos.astype(F32)[:, None] * inv[None, :]
    cos, sin = jnp.cos(ang), jnp.sin(ang)
    return jnp.concatenate([cos, cos, -sin, sin], axis=-1)


def _pad_front(state, rows):
    B, n, C = state.shape
    return jnp.concatenate([jnp.zeros((B, rows - n, C), state.dtype), state], axis=1)


def _tile(T, want):
    t = min(T, want)
    assert T % t == 0
    return t


def _layer(x, tab, cache, conv_prev, ffn_prev, lw, norm_final):
    B, T, D = x.shape
    if cache is None:
        q, kt, v, ckv, kpe, u, sg = _inproj(x, tab, lw, _tile(T, 512), latent=False)
        ap = _attn(q, kt, v)
    else:
        qlat, qpe, ckv, ckvb, kpe, kpeb, u, sg = _inproj(x, tab, lw, _tile(T, 512), latent=True)
        ap = _attn_cache(qlat, qpe, cache[0], cache[1], ckvb, kpeb, lw["wuv"])
    ca = _conv(u, _pad_front(conv_prev, CONV_HALO), lw, _tile(T, 512))
    outs = _outffn(x, ap, ca, sg, _pad_front(ffn_prev, FFN_HALO), lw, norm_final, _tile(T, 256))
    conv_state = jnp.concatenate([conv_prev, u], axis=1)[:, -(CONV_WIDTH - 1):]
    ffn_state = outs[1][:, -(FFN_CONV_WIDTH - 1):]
    y = outs[2] if norm_final is not None else None
    return outs[0], y, ckv, kpe, conv_state, ffn_state


def kernel(x_prompt, x_sample, cache_kv_latent, cache_k_rope, state_conv, state_ffn_conv, norm_mix, w_in, q_norm,
           w_qb, kv_norm, w_kvb, w_o_attn, conv_dw, conv_dw_b, conv_ln_g, conv_ln_b, w_conv_out, w_out, norm_ffn,
           w_up, ffn_dw, ffn_dw_b, w_down, norm_final):
    depth = w_in.shape[0]
    Bp, Tp, D = x_prompt.shape
    Bs, Ts, _ = x_sample.shape
    past = cache_kv_latent.shape[2]
    tab_p = _rope_table(jnp.arange(Tp))
    tab_s = _rope_table(past + jnp.arange(Ts))
    per_layer = [norm_mix, w_in, q_norm, w_qb, kv_norm, w_kvb, w_o_attn, conv_dw, conv_dw_b, conv_ln_g, conv_ln_b,
                 w_conv_out, w_out, norm_ffn, w_up, ffn_dw, ffn_dw_b, w_down]
    nfin = norm_final.reshape(1, -1)
    xp, xs = x_prompt, x_sample
    outs_p, outs_s = [], []
    yp = ys = None
    for l in range(depth):
        lw = _layer_weights(*[w[l] for w in per_layer])
        last = nfin if l == depth - 1 else None
        zc = jnp.zeros((Bp, CONV_WIDTH - 1, conv_dw.shape[2]), F32)
        zf = jnp.zeros((Bp, FFN_CONV_WIDTH - 1, w_up.shape[2]), F32)
        xp, yp, *sp = _layer(xp, tab_p, None, zc, zf, lw, last)
        xs, ys, *ss = _layer(xs, tab_s, (cache_kv_latent[l], cache_k_rope[l]), state_conv[l], state_ffn_conv[l],
                             lw, last)
        outs_p.append(sp)
        outs_s.append(ss)
    stack = lambda outs, j: jnp.stack([o[j] for o in outs])
    return (yp, ys,
            stack(outs_p, 0), stack(outs_p, 1), stack(outs_p, 2), stack(outs_p, 3),
            stack(outs_s, 0), stack(outs_s, 1), stack(outs_s, 2), stack(outs_s, 3))
```

```python
import functools
import math

import jax
import jax.numpy as jnp
from jax import lax
from jax.experimental import pallas as pl
from jax.experimental.pallas import tpu as pltpu

N_HEADS = 8
QK_NOPE_DIM = 128
QK_ROPE_DIM = 64
QK_DIM = QK_NOPE_DIM + QK_ROPE_DIM
V_HEAD_DIM = 128
Q_LORA_RANK = 384
KV_LORA_RANK = 256
CHUNK = 64
ROPE_THETA = 10000.0
ATTN_SCALE = QK_DIM ** -0.5
Q_SCALE = ATTN_SCALE * math.log2(math.e)
CONV_WIDTH = 31
FFN_CONV_WIDTH = 3
EPS = 1e-6

LANES = 128
SUBLANES = 8
VMEM_LIMIT_BYTES = 56 * 1024 * 1024
CONV_HALO = 32
FFN_HALO = SUBLANES
FFN_COLS = 256
ATTN_ROWS = 256
ATTN_QSTEP = 2048

BF16 = jnp.bfloat16
F32 = jnp.float32


def _dot(a, b):
    return jnp.dot(a, b, preferred_element_type=F32)


def _dot_nt(a, b):
    return lax.dot_general(a, b, (((1,), (1,)), ((), ())), preferred_element_type=F32)


def _rms(x, g):
    return x * lax.rsqrt(jnp.mean(x * x, axis=-1, keepdims=True) + EPS) * g


def _const_spec(shape):
    nd = len(shape)
    return pl.BlockSpec(shape, lambda *_: (0,) * nd, pipeline_mode=pl.Buffered(1))


def _params(n_axes):
    return pltpu.CompilerParams(dimension_semantics=("arbitrary",) * n_axes,
                                vmem_limit_bytes=VMEM_LIMIT_BYTES)


def _inproj_kernel(*refs, latent):
    (x_ref, tab_ref, nm_ref, wqk_ref, wkva_ref, wa_ref, wg_ref, wgt_ref, qn_ref, wqn_ref, wqr_ref,
     kvn_ref) = refs[:12]
    if latent:
        wuk_ref, qlat_ref, qpe_ref, ckv_ref, ckvb_ref, kpe_ref, kpeb_ref, u_ref, sg_ref = refs[12:]
    else:
        wukt_ref, wuv_ref, q_ref, kt_ref, v_ref, ckv_ref, kpe_ref, u_ref, sg_ref = refs[12:]
    x = x_ref[0]
    xn = _rms(x, nm_ref[...]).astype(BF16)
    tab = tab_ref[...]

    def rope(pair):
        t = pair * tab
        return t + pltpu.roll(t, QK_ROPE_DIM, 1)

    qk = _dot(xn, wqk_ref[...])
    qn = _rms(qk[:, :Q_LORA_RANK], qn_ref[...]).astype(BF16)
    q_nope = _dot(qn, wqn_ref[...])
    q_rope = _dot(qn, wqr_ref[...])
    ckv = _rms(_dot(xn, wkva_ref[...]), kvn_ref[...])
    kpe = rope(qk[:, Q_LORA_RANK:])
    ckv_ref[0] = ckv
    kpe_ref[0] = kpe[:, :QK_ROPE_DIM]
    ckvb = ckv.astype(BF16)

    if latent:
        ckvb_ref[0] = ckvb
        kpeb_ref[0] = kpe[:, :QK_ROPE_DIM].astype(BF16)
        for h in range(N_HEADS):
            cols = slice(h * LANES, (h + 1) * LANES)
            q_lat = _dot(q_nope[:, cols].astype(BF16), wuk_ref[h]) * Q_SCALE
            qlat_ref[0, h] = q_lat.astype(BF16)
            qpe_ref[0, h] = (rope(q_rope[:, cols])[:, :QK_ROPE_DIM] * Q_SCALE).astype(BF16)
    else:
        knt = _dot(wukt_ref[...], ckv.T.astype(BF16))
        kpet = kpe.T[:QK_ROPE_DIM, :].astype(BF16)
        v_all = _dot(ckvb, wuv_ref[...])
        for h in range(N_HEADS):
            cols = slice(h * LANES, (h + 1) * LANES)
            kt_ref[0, h, 0, 0:QK_NOPE_DIM, :] = knt[h * QK_NOPE_DIM:(h + 1) * QK_NOPE_DIM, :].astype(BF16)
            kt_ref[0, h, 0, QK_NOPE_DIM:QK_DIM, :] = kpet
            v_ref[0, h] = v_all[:, cols].astype(BF16)
            q_ref[0, h, :, 0:QK_NOPE_DIM] = (q_nope[:, cols] * Q_SCALE).astype(BF16)
            q_ref[0, h, :, QK_NOPE_DIM:QK_DIM] = (rope(q_rope[:, cols])[:, :QK_ROPE_DIM] * Q_SCALE).astype(BF16)

    u_ref[0] = _dot(xn, wa_ref[...]) * jax.nn.sigmoid(_dot(xn, wg_ref[...]))
    sg_ref[0] = jax.nn.sigmoid(_dot(xn, wgt_ref[...]))


def _inproj(x, tab, lw, tm, latent):
    B, T, D = x.shape
    H = N_HEADS
    C = lw["wa"].shape[1]
    row = lambda b, t: (b, t, 0)
    head = lambda b, t: (b, 0, t, 0)
    in_specs = [pl.BlockSpec((1, tm, D), row),
                pl.BlockSpec((tm, LANES), lambda b, t: (t, 0))]
    weights = [lw["norm_mix"], lw["wqk"], lw["wkva"], lw["wa"], lw["wg"], lw["wgt"],
               lw["q_norm"], lw["wqn"], lw["wqr"], lw["kv_norm"]]
    weights += [lw["wuk"]] if latent else [lw["wukt"], lw["wuv_all"]]
    in_specs += [_const_spec(w.shape) for w in weights]
    f32_rows = lambda n: (jax.ShapeDtypeStruct((B, T, n), F32), pl.BlockSpec((1, tm, n), row))
    bf16_rows = lambda n: (jax.ShapeDtypeStruct((B, T, n), BF16), pl.BlockSpec((1, tm, n), row))
    bf16_heads = lambda n: (jax.ShapeDtypeStruct((B, H, T, n), BF16), pl.BlockSpec((1, H, tm, n), head))
    if latent:
        outs = [bf16_heads(KV_LORA_RANK), bf16_heads(QK_ROPE_DIM), f32_rows(KV_LORA_RANK), bf16_rows(KV_LORA_RANK),
                f32_rows(QK_ROPE_DIM), bf16_rows(QK_ROPE_DIM)]
    else:
        keys_t = (jax.ShapeDtypeStruct((B, H, T // tm, QK_DIM, tm), BF16),
                  pl.BlockSpec((1, H, 1, QK_DIM, tm), lambda b, t: (b, 0, t, 0, 0)))
        outs = [bf16_heads(QK_DIM), keys_t, bf16_heads(V_HEAD_DIM), f32_rows(KV_LORA_RANK), f32_rows(QK_ROPE_DIM)]
    outs += [f32_rows(C), f32_rows(2 * D)]
    return pl.pallas_call(functools.partial(_inproj_kernel, latent=latent),
                          out_shape=[o[0] for o in outs], grid=(B, T // tm), in_specs=in_specs,
                          out_specs=[o[1] for o in outs], compiler_params=_params(2),
                          name="inproj_latent" if latent else "inproj")(x, tab, *weights)


def _conv_kernel(u_ref, prev_ref, w_ref, b_ref, lg_ref, lb_ref, o_ref, ubuf, ybuf, *, tc, rb):
    t = pl.program_id(1)
    n_slabs = ubuf.shape[0]
    for c in range(n_slabs):
        cols = slice(c * LANES, (c + 1) * LANES)

        @pl.when(t == 0)
        def _():
            ubuf[c, 0:CONV_HALO, :] = prev_ref[0, :, cols]

        @pl.when(t > 0)
        def _():
            ubuf[c, 0:CONV_HALO, :] = ubuf[c, tc:tc + CONV_HALO, :]

        ubuf[c, CONV_HALO:CONV_HALO + tc, :] = u_ref[0, :, cols]

    first = CONV_HALO - (CONV_WIDTH - 1)
    for c in range(n_slabs):
        cols = slice(c * LANES, (c + 1) * LANES)
        taps = [jnp.broadcast_to(w_ref[k:k + 1, cols], (rb, LANES)) for k in range(CONV_WIDTH)]
        bias = jnp.broadcast_to(b_ref[:, cols], (rb, LANES))

        def conv_rows(r, carry):
            r0 = r * (2 * rb)
            for phase in range(2):
                acc = bias
                for k in range(CONV_WIDTH):
                    acc = acc + taps[k] * ubuf[c, pl.ds(r0 + phase + first + k, rb, stride=2), :]
                ybuf[c, pl.ds(r0 + phase, rb, stride=2), :] = acc
            return carry

        lax.fori_loop(0, tc // (2 * rb), conv_rows, 0)

    lg = lg_ref[...]
    lb = lb_ref[...]
    nb = min(tc, 256)

    def norm_rows(r, carry):
        r0 = pl.multiple_of(r * nb, nb)
        y = jnp.concatenate([ybuf[c, pl.ds(r0, nb), :] for c in range(n_slabs)], axis=-1)
        d = y - jnp.mean(y, axis=-1, keepdims=True)
        yn = d * lax.rsqrt(jnp.mean(d * d, axis=-1, keepdims=True) + EPS) * lg + lb
        o_ref[0, pl.ds(r0, nb), :] = (yn * jax.nn.sigmoid(yn)).astype(BF16)
        return carry

    lax.fori_loop(0, tc // nb, norm_rows, 0)


def _conv(u, prev, lw, tc):
    B, T, C = u.shape
    assert tc >= CONV_HALO or T == tc
    rb = min(64, tc // 2)
    n_slabs = C // LANES
    row = lambda b, t: (b, t, 0)
    weights = [lw["conv_dw"], lw["conv_dw_b"], lw["conv_ln_g"], lw["conv_ln_b"]]
    return pl.pallas_call(
        functools.partial(_conv_kernel, tc=tc, rb=rb),
        out_shape=jax.ShapeDtypeStruct((B, T, C), BF16),
        grid=(B, T // tc),
        in_specs=[pl.BlockSpec((1, tc, C), row),
                  pl.BlockSpec((1, CONV_HALO, C), lambda b, t: (b, 0, 0))]
                 + [_const_spec(w.shape) for w in weights],
        out_specs=pl.BlockSpec((1, tc, C), row),
        scratch_shapes=[pltpu.VMEM((n_slabs, tc + CONV_HALO, LANES), F32), pltpu.VMEM((n_slabs, tc, LANES), F32)],
        compiler_params=_params(2), name="dwconv")(u, prev, *weights)


def _attn_kernel(q_ref, kt_ref, v_ref, o_ref, m_sc, acc_sc, *, tq, rg, kd):
    T = q_ref.shape[2]
    G = tq // rg
    pieces_per_step = tq // kd

    def ones_col(rows):
        return jnp.where(lax.broadcasted_iota(jnp.int32, (rows, LANES), 1) == 0, 1.0, 0.0).astype(BF16)

    q_chunk = lax.broadcasted_iota(jnp.int32, (rg, rg), 0) // CHUNK
    k_chunk = lax.broadcasted_iota(jnp.int32, (rg, rg), 1) // CHUNK
    own_mask = jnp.where(k_chunk <= q_chunk, 0.0, -jnp.inf).astype(F32)

    def scores(g, q0, key_pieces):
        q = q_ref[0, 0, pl.ds(q0 + g * rg, rg), :]
        return jnp.concatenate([_dot(q, kt) for kt in key_pieces], axis=1)

    def update(g, s, v):
        width = s.shape[1]
        m_prev = m_sc[g]
        m_new = jnp.maximum(m_prev, jnp.max(s, axis=-1, keepdims=True))
        alpha = jnp.exp2(m_prev - m_new)
        p = jnp.exp2(s - jnp.tile(m_new, (1, width // LANES))).astype(BF16)
        pv = _dot(p, jnp.concatenate([v, ones_col(width)], axis=1))
        acc_sc[g] = jnp.tile(alpha, (1, 2)) * acc_sc[g] + pv
        m_sc[g] = m_new

    def run_groups(score_fn, value_fn):
        s_next = score_fn(0)
        for g in range(G):
            s = s_next
            if g + 1 < G:
                s_next = score_fn(g + 1)
            update(g, s, value_fn(g))

    def qstep(i, carry):
        q0 = pl.multiple_of(i * tq, tq)
        m_sc[...] = jnp.full(m_sc.shape, -jnp.inf, F32)
        acc_sc[...] = jnp.zeros(acc_sc.shape, F32)

        def unmasked_block(k, c):
            key_pieces = [kt_ref[0, 0, k * pieces_per_step + j] for j in range(pieces_per_step)]
            v = v_ref[0, 0, pl.ds(pl.multiple_of(k * tq, tq), tq), :]
            run_groups(lambda g: scores(g, q0, key_pieces), lambda g: v)
            return c

        lax.fori_loop(0, i, unmasked_block, 0)

        def own_scores(g):
            width = (g + 1) * rg
            key_pieces = [kt_ref[0, 0, i * pieces_per_step + j] for j in range(width // kd)]
            if width % kd:
                key_pieces.append(kt_ref[0, 0, i * pieces_per_step + width // kd][:, :width % kd])
            s = scores(g, q0, key_pieces)
            own = s[:, width - rg:] + own_mask
            return jnp.concatenate([s[:, :width - rg], own], axis=1) if g else own

        run_groups(own_scores, lambda g: v_ref[0, 0, pl.ds(q0, (g + 1) * rg), :])

        for g in range(G):
            a = acc_sc[g]
            o_ref[0, pl.ds(q0 + g * rg, rg), :] = (a[:, :V_HEAD_DIM] / a[:, V_HEAD_DIM:V_HEAD_DIM + 1]).astype(BF16)
        return carry

    lax.fori_loop(0, T // tq, qstep, 0)


def _attn(q, kt, v):
    B, H, T, _ = q.shape
    kd = kt.shape[-1]
    tq = min(T, ATTN_QSTEP)
    rg = min(tq, ATTN_ROWS)
    assert T % tq == 0 and tq % rg == 0 and tq % kd == 0 and rg % CHUNK == 0 and (rg % kd == 0 or kd % rg == 0)
    G = tq // rg
    return pl.pallas_call(
        functools.partial(_attn_kernel, tq=tq, rg=rg, kd=kd),
        out_shape=jax.ShapeDtypeStruct((B, T, H * V_HEAD_DIM), BF16),
        grid=(B, H),
        in_specs=[pl.BlockSpec((1, 1, T, QK_DIM), lambda b, h: (b, h, 0, 0)),
                  pl.BlockSpec((1, 1, T // kd, QK_DIM, kd), lambda b, h: (b, h, 0, 0, 0)),
                  pl.BlockSpec((1, 1, T, V_HEAD_DIM), lambda b, h: (b, h, 0, 0))],
        out_specs=pl.BlockSpec((1, T, V_HEAD_DIM), lambda b, h: (b, 0, h)),
        scratch_shapes=[pltpu.VMEM((G, rg, LANES), F32), pltpu.VMEM((G, rg, 2 * LANES), F32)],
        compiler_params=_params(2), name="attn_prompt")(q, kt, v)


def _attn_cache_kernel(ql_ref, qp_ref, clat_ref, ckpe_ref, nkv_ref, nkpe_ref, wuv_ref, o_ref, kall, pall,
                       *, ts, past):
    M = N_HEADS * ts
    n_keys = kall.shape[0]
    ql = ql_ref[0].reshape(M, KV_LORA_RANK)
    qp = qp_ref[0].reshape(M, QK_ROPE_DIM)
    kall[0:past, :] = clat_ref[0].astype(BF16)
    pall[0:past, :] = ckpe_ref[0].astype(BF16)
    kall[past:n_keys, :] = jnp.zeros((n_keys - past, KV_LORA_RANK), BF16)
    pall[past:n_keys, :] = jnp.zeros((n_keys - past, QK_ROPE_DIM), BF16)
    kall[past:past + ts, :] = nkv_ref[0]
    pall[past:past + ts, :] = nkpe_ref[0]
    kc = kall[...]
    s = _dot_nt(ql, kc) + _dot_nt(qp, pall[...])
    valid = lax.broadcasted_iota(jnp.int32, s.shape, 1) < past + ts
    s = jnp.where(valid, s, -jnp.inf)
    p = jnp.exp2(s - jnp.max(s, axis=-1, keepdims=True))
    o = _dot(p.astype(BF16), kc) / jnp.sum(p, axis=-1, keepdims=True)
    for h in range(N_HEADS):
        oh = o[h * ts:(h + 1) * ts, :].astype(BF16)
        o_ref[0, :, h * V_HEAD_DIM:(h + 1) * V_HEAD_DIM] = _dot(oh, wuv_ref[h]).astype(BF16)


def _attn_cache(qlat, qpe, cache_lat, cache_kpe, nkv, nkpe, wuv):
    B, H, ts, _ = qlat.shape
    past = cache_lat.shape[1]
    n_keys = past + LANES
    assert ts <= LANES
    b4 = lambda b: (b, 0, 0, 0)
    b3 = lambda b: (b, 0, 0)
    return pl.pallas_call(
        functools.partial(_attn_cache_kernel, ts=ts, past=past),
        out_shape=jax.ShapeDtypeStruct((B, ts, H * V_HEAD_DIM), BF16),
        grid=(B,),
        in_specs=[pl.BlockSpec((1, H, ts, KV_LORA_RANK), b4),
                  pl.BlockSpec((1, H, ts, QK_ROPE_DIM), b4),
                  pl.BlockSpec((1, past, KV_LORA_RANK), b3),
                  pl.BlockSpec((1, past, QK_ROPE_DIM), b3),
                  pl.BlockSpec((1, ts, KV_LORA_RANK), b3),
                  pl.BlockSpec((1, ts, QK_ROPE_DIM), b3),
                  _const_spec(wuv.shape)],
        out_specs=pl.BlockSpec((1, ts, H * V_HEAD_DIM), b3),
        scratch_shapes=[pltpu.VMEM((n_keys, KV_LORA_RANK), BF16), pltpu.VMEM((n_keys, QK_ROPE_DIM), BF16)],
        compiler_params=_params(1), name="attn_cache")(qlat, qpe, cache_lat, cache_kpe, nkv, nkpe, wuv)


def _outffn_kernel(*refs, tm, final):
    (x_ref, ap_ref, ca_ref, sg_ref, prev_ref, woa_ref, wco_ref, wout_ref, nf_ref, wup_ref, fw_ref, fb_ref,
     wdn_ref) = refs[:13]
    if final:
        nfin_ref, xo_ref, st_ref, y_ref, upbuf, carry, accbuf = refs[13:]
    else:
        xo_ref, st_ref, upbuf, carry, accbuf = refs[13:]
    t = pl.program_id(1)
    D = x_ref.shape[-1]
    d_ff = wdn_ref.shape[0]
    n_steps = d_ff // FFN_COLS
    slabs = FFN_COLS // LANES
    half = tm // 2

    sg = sg_ref[0]
    merged = sg[:, :D] * _dot(ap_ref[0], woa_ref[...]) + sg[:, D:] * _dot(ca_ref[0], wco_ref[...])
    x1 = x_ref[0] + _dot(merged.astype(BF16), wout_ref[...])
    xn = _rms(x1, nf_ref[...]).astype(BF16)

    @pl.when(t == 0)
    def _():
        carry[...] = prev_ref[0]

    def up_project(c):
        for part, base in enumerate((c * FFN_COLS, d_ff + c * FFN_COLS)):
            up = _dot(xn, wup_ref[:, base:base + FFN_COLS])
            for sl in range(slabs):
                cols = slice(base + sl * LANES, base + (sl + 1) * LANES)
                idx = ((c % 2) * 2 + part) * slabs + sl
                upbuf[idx, 0:FFN_HALO, :] = carry[:, cols]
                upbuf[idx, FFN_HALO:FFN_HALO + tm, :] = up[:, sl * LANES:(sl + 1) * LANES]
                carry[:, cols] = upbuf[idx, tm:tm + FFN_HALO, :]

    def conv3(c, part):
        base = (c * FFN_COLS, d_ff + c * FFN_COLS)[part]
        out = []
        for sl in range(slabs):
            cols = slice(base + sl * LANES, base + (sl + 1) * LANES)
            idx = ((c % 2) * 2 + part) * slabs + sl
            halves = []
            for phase in range(2):
                y = jnp.broadcast_to(fb_ref[:, cols], (half, LANES))
                for k in range(FFN_CONV_WIDTH):
                    first = FFN_HALO - (FFN_CONV_WIDTH - 1) + k + phase
                    y = y + fw_ref[k:k + 1, cols] * upbuf[idx, pl.ds(first, half, stride=2), :]
                halves.append(y)
            out.append(jnp.concatenate(halves, axis=0))
        return jnp.concatenate(out, axis=1)

    acc = jnp.zeros((tm, D), F32)
    up_project(0)
    for c in range(n_steps):
        if c + 1 < n_steps:
            up_project(c + 1)
        ga = conv3(c, 0)
        v = conv3(c, 1)
        act = (ga * jax.nn.sigmoid(ga) * v).astype(BF16)
        acc = acc + _dot(act, wdn_ref[c * FFN_COLS:(c + 1) * FFN_COLS, :])

    for sl in range(D // LANES):
        cols = slice(sl * LANES, (sl + 1) * LANES)
        accbuf[sl, pl.ds(0, half, stride=2), :] = acc[:half, cols]
        accbuf[sl, pl.ds(1, half, stride=2), :] = acc[half:, cols]
    x2 = x1 + jnp.concatenate([accbuf[sl] for sl in range(D // LANES)], axis=1)
    xo_ref[0] = x2
    st_ref[0] = carry[...]
    if final:
        y_ref[0] = _rms(x2, nfin_ref[...])


def _outffn(x, ap, ca, sg, prev, lw, norm_final, tm):
    B, T, D = x.shape
    d_up = lw["wup"].shape[1]
    final = norm_final is not None
    row = lambda b, t: (b, t, 0)
    per_batch = lambda b, t: (b, 0, 0)
    weights = [lw["woa"], lw["wco"], lw["wout"], lw["norm_ffn"], lw["wup"], lw["ffn_dw"], lw["ffn_dw_b"], lw["wdn"]]
    if final:
        weights.append(norm_final)
    in_specs = [pl.BlockSpec((1, tm, D), row), pl.BlockSpec((1, tm, D), row), pl.BlockSpec((1, tm, D), row),
                pl.BlockSpec((1, tm, 2 * D), row), pl.BlockSpec((1, FFN_HALO, d_up), per_batch)]
    in_specs += [_const_spec(w.shape) for w in weights]
    out_shape = [jax.ShapeDtypeStruct((B, T, D), F32), jax.ShapeDtypeStruct((B, FFN_HALO, d_up), F32)]
    out_specs = [pl.BlockSpec((1, tm, D), row), pl.BlockSpec((1, FFN_HALO, d_up), per_batch)]
    if final:
        out_shape.append(jax.ShapeDtypeStruct((B, T, D), F32))
        out_specs.append(pl.BlockSpec((1, tm, D), row))
    return pl.pallas_call(
        functools.partial(_outffn_kernel, tm=tm, final=final),
        out_shape=out_shape, grid=(B, T // tm), in_specs=in_specs, out_specs=out_specs,
        scratch_shapes=[pltpu.VMEM((4 * FFN_COLS // LANES, tm + FFN_HALO, LANES), F32),
                        pltpu.VMEM((FFN_HALO, d_up), F32),
                        pltpu.VMEM((D // LANES, tm, LANES), F32)],
        compiler_params=_params(2), name="outffn_final" if final else "outffn")(x, ap, ca, sg, prev, *weights)


def _swap_halves(w):
    half = w.shape[-1] // 2
    return jnp.concatenate([w[..., half:], w[..., :half]], axis=-1)


def _layer_weights(norm_mix, w_in, q_norm, w_qb, kv_norm, w_kvb, w_o_attn, conv_dw, conv_dw_b, conv_ln_g,
                   conv_ln_b, w_conv_out, w_out, norm_ffn, w_up, ffn_dw, ffn_dw_b, w_down):
    C = conv_dw.shape[1]
    H = N_HEADS
    o0 = Q_LORA_RANK
    o1 = o0 + KV_LORA_RANK
    o2 = o1 + QK_ROPE_DIM
    o3 = o2 + C
    o4 = o3 + C
    w_kpe = w_in[:, o1:o2]
    w_qb_h = w_qb.reshape(Q_LORA_RANK, H, QK_DIM)
    w_q_rope = w_qb_h[..., QK_NOPE_DIM:]
    w_kvb_h = w_kvb.reshape(KV_LORA_RANK, H, QK_NOPE_DIM + V_HEAD_DIM)
    w_uk = w_kvb_h[..., :QK_NOPE_DIM]
    w_uv = w_kvb_h[..., QK_NOPE_DIM:]
    row = lambda v: v.reshape(1, -1)
    return dict(
        norm_mix=row(norm_mix), q_norm=row(q_norm), kv_norm=row(kv_norm), norm_ffn=row(norm_ffn),
        wqk=jnp.concatenate([w_in[:, :o0], w_kpe, _swap_halves(w_kpe)], axis=-1).astype(BF16),
        wkva=w_in[:, o0:o1].astype(BF16),
        wa=w_in[:, o2:o3].astype(BF16), wg=w_in[:, o3:o4].astype(BF16), wgt=w_in[:, o4:].astype(BF16),
        wqn=w_qb_h[..., :QK_NOPE_DIM].reshape(Q_LORA_RANK, H * QK_NOPE_DIM).astype(BF16),
        wqr=jnp.concatenate([w_q_rope, _swap_halves(w_q_rope)], axis=-1).reshape(Q_LORA_RANK, H * LANES).astype(BF16),
        wuk=jnp.transpose(w_uk, (1, 2, 0)).astype(BF16),
        wuv=jnp.transpose(w_uv, (1, 0, 2)).astype(BF16),
        wukt=jnp.transpose(w_uk, (1, 2, 0)).reshape(H * QK_NOPE_DIM, KV_LORA_RANK).astype(BF16),
        wuv_all=w_uv.reshape(KV_LORA_RANK, H * V_HEAD_DIM).astype(BF16),
        woa=w_o_attn.astype(BF16), wco=w_conv_out.astype(BF16), wout=w_out.astype(BF16),
        conv_dw=conv_dw, conv_dw_b=row(conv_dw_b), conv_ln_g=row(conv_ln_g), conv_ln_b=row(conv_ln_b),
        wup=w_up.astype(BF16), ffn_dw=ffn_dw, ffn_dw_b=row(ffn_dw_b), wdn=w_down.astype(BF16))


def _rope_table(pos):
    half = QK_ROPE_DIM // 2
    inv = jnp.power(ROPE_THETA, -jnp.arange(half, dtype=F32) / half)
    ang = pos.astype(F32)[:, None] * inv[None, :]
    cos, sin = jnp.cos(ang), jnp.sin(ang)
    return jnp.concatenate([cos, cos, -sin, sin], axis=-1)


def _pad_front(state, rows):
    B, n, C = state.shape
    return jnp.concatenate([jnp.zeros((B, rows - n, C), state.dtype), state], axis=1)


def _tile(T, want):
    t = min(T, want)
    assert T % t == 0
    return t


def _layer(x, tab, cache, conv_prev, ffn_prev, lw, norm_final):
    B, T, D = x.shape
    if cache is None:
        q, kt, v, ckv, kpe, u, sg = _inproj(x, tab, lw, _tile(T, 512), latent=False)
        ap = _attn(q, kt, v)
    else:
        qlat, qpe, ckv, ckvb, kpe, kpeb, u, sg = _inproj(x, tab, lw, _tile(T, 512), latent=True)
        ap = _attn_cache(qlat, qpe, cache[0], cache[1], ckvb, kpeb, lw["wuv"])
    ca = _conv(u, _pad_front(conv_prev, CONV_HALO), lw, _tile(T, 512))
    ffn_rows = 512 if norm_final is None else 256
    outs = _outffn(x, ap, ca, sg, _pad_front(ffn_prev, FFN_HALO), lw, norm_final, _tile(T, ffn_rows))
    conv_state = jnp.concatenate([conv_prev, u], axis=1)[:, -(CONV_WIDTH - 1):]
    ffn_state = outs[1][:, -(FFN_CONV_WIDTH - 1):]
    y = outs[2] if norm_final is not None else None
    return outs[0], y, ckv, kpe, conv_state, ffn_state


def kernel(x_prompt, x_sample, cache_kv_latent, cache_k_rope, state_conv, state_ffn_conv, norm_mix, w_in, q_norm,
           w_qb, kv_norm, w_kvb, w_o_attn, conv_dw, conv_dw_b, conv_ln_g, conv_ln_b, w_conv_out, w_out, norm_ffn,
           w_up, ffn_dw, ffn_dw_b, w_down, norm_final):
    depth = w_in.shape[0]
    Bp, Tp, D = x_prompt.shape
    Bs, Ts, _ = x_sample.shape
    past = cache_kv_latent.shape[2]
    tab_p = _rope_table(jnp.arange(Tp))
    tab_s = _rope_table(past + jnp.arange(Ts))
    per_layer = [norm_mix, w_in, q_norm, w_qb, kv_norm, w_kvb, w_o_attn, conv_dw, conv_dw_b, conv_ln_g, conv_ln_b,
                 w_conv_out, w_out, norm_ffn, w_up, ffn_dw, ffn_dw_b, w_down]
    nfin = norm_final.reshape(1, -1)
    xp, xs = x_prompt, x_sample
    outs_p, outs_s = [], []
    yp = ys = None
    for l in range(depth):
        lw = _layer_weights(*[w[l] for w in per_layer])
        last = nfin if l == depth - 1 else None
        zc = jnp.zeros((Bp, CONV_WIDTH - 1, conv_dw.shape[2]), F32)
        zf = jnp.zeros((Bp, FFN_CONV_WIDTH - 1, w_up.shape[2]), F32)
        xp, yp, *sp = _layer(xp, tab_p, None, zc, zf, lw, last)
        xs, ys, *ss = _layer(xs, tab_s, (cache_kv_latent[l], cache_k_rope[l]), state_conv[l], state_ffn_conv[l],
                             lw, last)
        outs_p.append(sp)
        outs_s.append(ss)
    stack = lambda outs, j: jnp.stack([o[j] for o in outs])
    return (yp, ys,
            stack(outs_p, 0), stack(outs_p, 1), stack(outs_p, 2), stack(outs_p, 3),
            stack(outs_s, 0), stack(outs_s, 1), stack(outs_s, 2), stack(outs_s, 3))
```

```python
import functools
import math

import jax
import jax.numpy as jnp
from jax import lax
from jax.experimental import pallas as pl
from jax.experimental.pallas import tpu as pltpu

N_HEADS = 8
QK_NOPE_DIM = 128
QK_ROPE_DIM = 64
QK_DIM = QK_NOPE_DIM + QK_ROPE_DIM
V_HEAD_DIM = 128
Q_LORA_RANK = 384
KV_LORA_RANK = 256
CHUNK = 64
ROPE_THETA = 10000.0
ATTN_SCALE = QK_DIM ** -0.5
Q_SCALE = ATTN_SCALE * math.log2(math.e)
CONV_WIDTH = 31
FFN_CONV_WIDTH = 3
EPS = 1e-6

LANES = 128
SUBLANES = 8
VMEM_LIMIT_BYTES = 56 * 1024 * 1024
CONV_HALO = 32
FFN_HALO = SUBLANES
FFN_COLS = 256
ATTN_ROWS = 256
ATTN_QSTEP = 2048
TILE_ROWS = 512
CONV_BLOCK_ROWS = 64
NORM_BLOCK_ROWS = 256

BF16 = jnp.bfloat16
F32 = jnp.float32


def _dot(a, b):
    return jnp.dot(a, b, preferred_element_type=F32)


def _dot_nt(a, b):
    return lax.dot_general(a, b, (((1,), (1,)), ((), ())), preferred_element_type=F32)


def _rms(x, g):
    return x * lax.rsqrt(jnp.mean(x * x, axis=-1, keepdims=True) + EPS) * g


def _const_spec(shape):
    nd = len(shape)
    return pl.BlockSpec(shape, lambda *_: (0,) * nd, pipeline_mode=pl.Buffered(1))


def _params(n_axes):
    return pltpu.CompilerParams(dimension_semantics=("arbitrary",) * n_axes,
                                vmem_limit_bytes=VMEM_LIMIT_BYTES)


def _inproj_kernel(*refs, latent):
    x_ref, tab_ref, nm_ref, wqk_ref, wkva_ref, wcg_ref, qn_ref, wqn_ref, wqr_ref, kvn_ref = refs[:10]
    if latent:
        wuk_ref, qlat_ref, qpe_ref, ckv_ref, ckvb_ref, kpe_ref, kpeb_ref, u_ref, sg_ref = refs[10:]
    else:
        wukt_ref, wuv_ref, q_ref, kt_ref, v_ref, ckv_ref, kpe_ref, u_ref, sg_ref = refs[10:]
    x = x_ref[0]
    xn = _rms(x, nm_ref[...]).astype(BF16)
    tab = tab_ref[...]

    def rope(pair):
        t = pair * tab
        return t + pltpu.roll(t, QK_ROPE_DIM, 1)

    qk = _dot(xn, wqk_ref[...])
    qn = _rms(qk[:, :Q_LORA_RANK], qn_ref[...]).astype(BF16)
    q_nope = _dot(qn, wqn_ref[...])
    q_rope = _dot(qn, wqr_ref[...])
    ckv = _rms(_dot(xn, wkva_ref[...]), kvn_ref[...])
    kpe = rope(qk[:, Q_LORA_RANK:])
    ckv_ref[0] = ckv
    kpe_ref[0] = kpe[:, :QK_ROPE_DIM]
    ckvb = ckv.astype(BF16)

    if latent:
        ckvb_ref[0] = ckvb
        kpeb_ref[0] = kpe[:, :QK_ROPE_DIM].astype(BF16)
        for h in range(N_HEADS):
            cols = slice(h * LANES, (h + 1) * LANES)
            q_lat = _dot(q_nope[:, cols].astype(BF16), wuk_ref[h]) * Q_SCALE
            qlat_ref[0, h] = q_lat.astype(BF16)
            qpe_ref[0, h] = (rope(q_rope[:, cols])[:, :QK_ROPE_DIM] * Q_SCALE).astype(BF16)
    else:
        knt = _dot(wukt_ref[...], ckv.T.astype(BF16))
        kpet = kpe.T[:QK_ROPE_DIM, :].astype(BF16)
        v_all = _dot(ckvb, wuv_ref[...])
        for h in range(N_HEADS):
            cols = slice(h * LANES, (h + 1) * LANES)
            kt_ref[0, h, 0, 0:QK_NOPE_DIM, :] = knt[h * QK_NOPE_DIM:(h + 1) * QK_NOPE_DIM, :].astype(BF16)
            kt_ref[0, h, 0, QK_NOPE_DIM:QK_DIM, :] = kpet
            v_ref[0, h] = v_all[:, cols].astype(BF16)
            q_ref[0, h, :, 0:QK_NOPE_DIM] = (q_nope[:, cols] * Q_SCALE).astype(BF16)
            q_ref[0, h, :, QK_NOPE_DIM:QK_DIM] = (rope(q_rope[:, cols])[:, :QK_ROPE_DIM] * Q_SCALE).astype(BF16)

    C = u_ref.shape[-1]
    cg = _dot(xn, wcg_ref[...])
    u_ref[0] = cg[:, :C] * jax.nn.sigmoid(cg[:, C:2 * C])
    sg_ref[0] = jax.nn.sigmoid(cg[:, 2 * C:])


def _inproj(x, tab, lw, tm, latent):
    B, T, D = x.shape
    H = N_HEADS
    C = (lw["wcg"].shape[1] - 2 * D) // 2
    row = lambda b, t: (b, t, 0)
    head = lambda b, t: (b, 0, t, 0)
    in_specs = [pl.BlockSpec((1, tm, D), row),
                pl.BlockSpec((tm, LANES), lambda b, t: (t, 0))]
    weights = [lw["norm_mix"], lw["wqk"], lw["wkva"], lw["wcg"],
               lw["q_norm"], lw["wqn"], lw["wqr"], lw["kv_norm"]]
    weights += [lw["wuk"]] if latent else [lw["wukt"], lw["wuv_all"]]
    in_specs += [_const_spec(w.shape) for w in weights]
    f32_rows = lambda n: (jax.ShapeDtypeStruct((B, T, n), F32), pl.BlockSpec((1, tm, n), row))
    bf16_rows = lambda n: (jax.ShapeDtypeStruct((B, T, n), BF16), pl.BlockSpec((1, tm, n), row))
    bf16_heads = lambda n: (jax.ShapeDtypeStruct((B, H, T, n), BF16), pl.BlockSpec((1, H, tm, n), head))
    if latent:
        outs = [bf16_heads(KV_LORA_RANK), bf16_heads(QK_ROPE_DIM), f32_rows(KV_LORA_RANK), bf16_rows(KV_LORA_RANK),
                f32_rows(QK_ROPE_DIM), bf16_rows(QK_ROPE_DIM)]
    else:
        keys_t = (jax.ShapeDtypeStruct((B, H, T // tm, QK_DIM, tm), BF16),
                  pl.BlockSpec((1, H, 1, QK_DIM, tm), lambda b, t: (b, 0, t, 0, 0)))
        outs = [bf16_heads(QK_DIM), keys_t, bf16_heads(V_HEAD_DIM), f32_rows(KV_LORA_RANK), f32_rows(QK_ROPE_DIM)]
    outs += [f32_rows(C), f32_rows(2 * D)]
    return pl.pallas_call(functools.partial(_inproj_kernel, latent=latent),
                          out_shape=[o[0] for o in outs], grid=(B, T // tm), in_specs=in_specs,
                          out_specs=[o[1] for o in outs], compiler_params=_params(2),
                          name="inproj_latent" if latent else "inproj")(x, tab, *weights)


def _conv_kernel(u_ref, prev_ref, w_ref, b_ref, lg_ref, lb_ref, o_ref, ubuf, ybuf, *, tc, rb):
    t = pl.program_id(1)
    n_slabs = ubuf.shape[0]
    for c in range(n_slabs):
        cols = slice(c * LANES, (c + 1) * LANES)

        @pl.when(t == 0)
        def _():
            ubuf[c, 0:CONV_HALO, :] = prev_ref[0, :, cols]

        @pl.when(t > 0)
        def _():
            ubuf[c, 0:CONV_HALO, :] = ubuf[c, tc:tc + CONV_HALO, :]

        ubuf[c, CONV_HALO:CONV_HALO + tc, :] = u_ref[0, :, cols]

    first = CONV_HALO - (CONV_WIDTH - 1)
    for c in range(n_slabs):
        cols = slice(c * LANES, (c + 1) * LANES)
        taps = [jnp.broadcast_to(w_ref[k:k + 1, cols], (rb, LANES)) for k in range(CONV_WIDTH)]
        bias = jnp.broadcast_to(b_ref[:, cols], (rb, LANES))

        def conv_rows(r, carry):
            r0 = r * (2 * rb)
            for phase in range(2):
                acc = bias
                for k in range(CONV_WIDTH):
                    acc = acc + taps[k] * ubuf[c, pl.ds(r0 + phase + first + k, rb, stride=2), :]
                ybuf[c, pl.ds(r0 + phase, rb, stride=2), :] = acc
            return carry

        lax.fori_loop(0, tc // (2 * rb), conv_rows, 0)

    lg = lg_ref[...]
    lb = lb_ref[...]
    nb = min(tc, NORM_BLOCK_ROWS)

    def norm_rows(r, carry):
        r0 = pl.multiple_of(r * nb, nb)
        y = jnp.concatenate([ybuf[c, pl.ds(r0, nb), :] for c in range(n_slabs)], axis=-1)
        d = y - jnp.mean(y, axis=-1, keepdims=True)
        yn = d * lax.rsqrt(jnp.mean(d * d, axis=-1, keepdims=True) + EPS) * lg + lb
        o_ref[0, pl.ds(r0, nb), :] = (yn * jax.nn.sigmoid(yn)).astype(BF16)
        return carry

    lax.fori_loop(0, tc // nb, norm_rows, 0)


def _conv(u, prev, lw, tc):
    B, T, C = u.shape
    assert tc >= CONV_HALO or T == tc
    rb = min(CONV_BLOCK_ROWS, tc // 2)
    n_slabs = C // LANES
    row = lambda b, t: (b, t, 0)
    weights = [lw["conv_dw"], lw["conv_dw_b"], lw["conv_ln_g"], lw["conv_ln_b"]]
    return pl.pallas_call(
        functools.partial(_conv_kernel, tc=tc, rb=rb),
        out_shape=jax.ShapeDtypeStruct((B, T, C), BF16),
        grid=(B, T // tc),
        in_specs=[pl.BlockSpec((1, tc, C), row),
                  pl.BlockSpec((1, CONV_HALO, C), lambda b, t: (b, 0, 0))]
                 + [_const_spec(w.shape) for w in weights],
        out_specs=pl.BlockSpec((1, tc, C), row),
        scratch_shapes=[pltpu.VMEM((n_slabs, tc + CONV_HALO, LANES), F32), pltpu.VMEM((n_slabs, tc, LANES), F32)],
        compiler_params=_params(2), name="dwconv")(u, prev, *weights)


def _attn_kernel(q_ref, kt_ref, v_ref, o_ref, m_sc, acc_sc, *, tq, rg, kd):
    T = q_ref.shape[2]
    G = tq // rg
    pieces_per_step = tq // kd

    def ones_col(rows):
        return jnp.where(lax.broadcasted_iota(jnp.int32, (rows, LANES), 1) == 0, 1.0, 0.0).astype(BF16)

    q_chunk = lax.broadcasted_iota(jnp.int32, (rg, rg), 0) // CHUNK
    k_chunk = lax.broadcasted_iota(jnp.int32, (rg, rg), 1) // CHUNK
    own_mask = jnp.where(k_chunk <= q_chunk, 0.0, -jnp.inf).astype(F32)

    def scores(g, q0, key_pieces):
        q = q_ref[0, 0, pl.ds(q0 + g * rg, rg), :]
        return jnp.concatenate([_dot(q, kt) for kt in key_pieces], axis=1)

    def update(g, s, v):
        width = s.shape[1]
        m_prev = m_sc[g]
        m_new = jnp.maximum(m_prev, jnp.max(s, axis=-1, keepdims=True))
        alpha = jnp.exp2(m_prev - m_new)
        p = jnp.exp2(s - jnp.tile(m_new, (1, width // LANES))).astype(BF16)
        pv = _dot(p, jnp.concatenate([v, ones_col(width)], axis=1))
        acc_sc[g] = jnp.tile(alpha, (1, 2)) * acc_sc[g] + pv
        m_sc[g] = m_new

    def run_groups(score_fn, value_fn):
        s_next = score_fn(0)
        for g in range(G):
            s = s_next
            if g + 1 < G:
                s_next = score_fn(g + 1)
            update(g, s, value_fn(g))

    def qstep(i, carry):
        q0 = pl.multiple_of(i * tq, tq)
        m_sc[...] = jnp.full(m_sc.shape, -jnp.inf, F32)
        acc_sc[...] = jnp.zeros(acc_sc.shape, F32)

        def unmasked_block(k, c):
            key_pieces = [kt_ref[0, 0, k * pieces_per_step + j] for j in range(pieces_per_step)]
            v = v_ref[0, 0, pl.ds(pl.multiple_of(k * tq, tq), tq), :]
            run_groups(lambda g: scores(g, q0, key_pieces), lambda g: v)
            return c

        lax.fori_loop(0, i, unmasked_block, 0)

        def own_scores(g):
            width = (g + 1) * rg
            key_pieces = [kt_ref[0, 0, i * pieces_per_step + j] for j in range(width // kd)]
            if width % kd:
                key_pieces.append(kt_ref[0, 0, i * pieces_per_step + width // kd][:, :width % kd])
            s = scores(g, q0, key_pieces)
            own = s[:, width - rg:] + own_mask
            return jnp.concatenate([s[:, :width - rg], own], axis=1) if g else own

        run_groups(own_scores, lambda g: v_ref[0, 0, pl.ds(q0, (g + 1) * rg), :])

        for g in range(G):
            a = acc_sc[g]
            o_ref[0, pl.ds(q0 + g * rg, rg), :] = (a[:, :V_HEAD_DIM] / a[:, V_HEAD_DIM:V_HEAD_DIM + 1]).astype(BF16)
        return carry

    lax.fori_loop(0, T // tq, qstep, 0)


def _attn(q, kt, v):
    B, H, T, _ = q.shape
    kd = kt.shape[-1]
    tq = min(T, ATTN_QSTEP)
    rg = min(tq, ATTN_ROWS)
    assert T % tq == 0 and tq % rg == 0 and tq % kd == 0 and rg % CHUNK == 0 and (rg % kd == 0 or kd % rg == 0)
    G = tq // rg
    return pl.pallas_call(
        functools.partial(_attn_kernel, tq=tq, rg=rg, kd=kd),
        out_shape=jax.ShapeDtypeStruct((B, T, H * V_HEAD_DIM), BF16),
        grid=(B, H),
        in_specs=[pl.BlockSpec((1, 1, T, QK_DIM), lambda b, h: (b, h, 0, 0)),
                  pl.BlockSpec((1, 1, T // kd, QK_DIM, kd), lambda b, h: (b, h, 0, 0, 0)),
                  pl.BlockSpec((1, 1, T, V_HEAD_DIM), lambda b, h: (b, h, 0, 0))],
        out_specs=pl.BlockSpec((1, T, V_HEAD_DIM), lambda b, h: (b, 0, h)),
        scratch_shapes=[pltpu.VMEM((G, rg, LANES), F32), pltpu.VMEM((G, rg, 2 * LANES), F32)],
        compiler_params=_params(2), name="attn_prompt")(q, kt, v)


def _attn_cache_kernel(ql_ref, qp_ref, clat_ref, ckpe_ref, nkv_ref, nkpe_ref, wuv_ref, o_ref, kall, pall,
                       *, ts, past):
    M = N_HEADS * ts
    n_keys = kall.shape[0]
    ql = ql_ref[0].reshape(M, KV_LORA_RANK)
    qp = qp_ref[0].reshape(M, QK_ROPE_DIM)
    kall[0:past, :] = clat_ref[0].astype(BF16)
    pall[0:past, :] = ckpe_ref[0].astype(BF16)
    kall[past:n_keys, :] = jnp.zeros((n_keys - past, KV_LORA_RANK), BF16)
    pall[past:n_keys, :] = jnp.zeros((n_keys - past, QK_ROPE_DIM), BF16)
    kall[past:past + ts, :] = nkv_ref[0]
    pall[past:past + ts, :] = nkpe_ref[0]
    kc = kall[...]
    s = _dot_nt(ql, kc) + _dot_nt(qp, pall[...])
    valid = lax.broadcasted_iota(jnp.int32, s.shape, 1) < past + ts
    s = jnp.where(valid, s, -jnp.inf)
    p = jnp.exp2(s - jnp.max(s, axis=-1, keepdims=True))
    o = _dot(p.astype(BF16), kc) / jnp.sum(p, axis=-1, keepdims=True)
    for h in range(N_HEADS):
        oh = o[h * ts:(h + 1) * ts, :].astype(BF16)
        o_ref[0, :, h * V_HEAD_DIM:(h + 1) * V_HEAD_DIM] = _dot(oh, wuv_ref[h]).astype(BF16)


def _attn_cache(qlat, qpe, cache_lat, cache_kpe, nkv, nkpe, wuv):
    B, H, ts, _ = qlat.shape
    past = cache_lat.shape[1]
    n_keys = past + LANES
    assert ts <= LANES
    b4 = lambda b: (b, 0, 0, 0)
    b3 = lambda b: (b, 0, 0)
    return pl.pallas_call(
        functools.partial(_attn_cache_kernel, ts=ts, past=past),
        out_shape=jax.ShapeDtypeStruct((B, ts, H * V_HEAD_DIM), BF16),
        grid=(B,),
        in_specs=[pl.BlockSpec((1, H, ts, KV_LORA_RANK), b4),
                  pl.BlockSpec((1, H, ts, QK_ROPE_DIM), b4),
                  pl.BlockSpec((1, past, KV_LORA_RANK), b3),
                  pl.BlockSpec((1, past, QK_ROPE_DIM), b3),
                  pl.BlockSpec((1, ts, KV_LORA_RANK), b3),
                  pl.BlockSpec((1, ts, QK_ROPE_DIM), b3),
                  _const_spec(wuv.shape)],
        out_specs=pl.BlockSpec((1, ts, H * V_HEAD_DIM), b3),
        scratch_shapes=[pltpu.VMEM((n_keys, KV_LORA_RANK), BF16), pltpu.VMEM((n_keys, QK_ROPE_DIM), BF16)],
        compiler_params=_params(1), name="attn_cache")(qlat, qpe, cache_lat, cache_kpe, nkv, nkpe, wuv)


def _outffn_kernel(*refs, tm, final):
    (x_ref, ap_ref, ca_ref, sg_ref, prev_ref, woa_ref, wco_ref, wout_ref, nf_ref, wup_ref, fw_ref, fb_ref,
     wdn_ref) = refs[:13]
    if final:
        nfin_ref, xo_ref, st_ref, upbuf, carry, accbuf = refs[13:]
    else:
        xo_ref, st_ref, upbuf, carry, accbuf = refs[13:]
    t = pl.program_id(1)
    D = x_ref.shape[-1]
    d_ff = wdn_ref.shape[0]
    n_steps = d_ff // FFN_COLS
    slabs = FFN_COLS // LANES
    half = tm // 2

    sg = sg_ref[0]
    merged = sg[:, :D] * _dot(ap_ref[0], woa_ref[...]) + sg[:, D:] * _dot(ca_ref[0], wco_ref[...])
    x1 = x_ref[0] + _dot(merged.astype(BF16), wout_ref[...])
    xn = _rms(x1, nf_ref[...]).astype(BF16)

    @pl.when(t == 0)
    def _():
        carry[...] = prev_ref[0]

    def up_project(c):
        for part, base in enumerate((c * FFN_COLS, d_ff + c * FFN_COLS)):
            up = _dot(xn, wup_ref[:, base:base + FFN_COLS])
            for sl in range(slabs):
                cols = slice(base + sl * LANES, base + (sl + 1) * LANES)
                idx = ((c % 2) * 2 + part) * slabs + sl
                upbuf[idx, 0:FFN_HALO, :] = carry[:, cols]
                upbuf[idx, FFN_HALO:FFN_HALO + tm, :] = up[:, sl * LANES:(sl + 1) * LANES]
                carry[:, cols] = upbuf[idx, tm:tm + FFN_HALO, :]

    def conv3(c, part):
        base = (c * FFN_COLS, d_ff + c * FFN_COLS)[part]
        out = []
        for sl in range(slabs):
            cols = slice(base + sl * LANES, base + (sl + 1) * LANES)
            idx = ((c % 2) * 2 + part) * slabs + sl
            halves = []
            for phase in range(2):
                y = jnp.broadcast_to(fb_ref[:, cols], (half, LANES))
                for k in range(FFN_CONV_WIDTH):
                    first = FFN_HALO - (FFN_CONV_WIDTH - 1) + k + phase
                    y = y + fw_ref[k:k + 1, cols] * upbuf[idx, pl.ds(first, half, stride=2), :]
                halves.append(y)
            out.append(jnp.concatenate(halves, axis=0))
        return jnp.concatenate(out, axis=1)

    acc = jnp.zeros((tm, D), F32)
    up_project(0)
    for c in range(n_steps):
        if c + 1 < n_steps:
            up_project(c + 1)
        ga = conv3(c, 0)
        v = conv3(c, 1)
        act = (ga * jax.nn.sigmoid(ga) * v).astype(BF16)
        acc = acc + _dot(act, wdn_ref[c * FFN_COLS:(c + 1) * FFN_COLS, :])

    for sl in range(D // LANES):
        cols = slice(sl * LANES, (sl + 1) * LANES)
        accbuf[sl, pl.ds(0, half, stride=2), :] = acc[:half, cols]
        accbuf[sl, pl.ds(1, half, stride=2), :] = acc[half:, cols]
    x2 = x1 + jnp.concatenate([accbuf[sl] for sl in range(D // LANES)], axis=1)
    xo_ref[0] = _rms(x2, nfin_ref[...]) if final else x2
    st_ref[0] = carry[...]


def _outffn(x, ap, ca, sg, prev, lw, norm_final, tm):
    B, T, D = x.shape
    d_up = lw["wup"].shape[1]
    final = norm_final is not None
    row = lambda b, t: (b, t, 0)
    per_batch = lambda b, t: (b, 0, 0)
    weights = [lw["woa"], lw["wco"], lw["wout"], lw["norm_ffn"], lw["wup"], lw["ffn_dw"], lw["ffn_dw_b"], lw["wdn"]]
    if final:
        weights.append(norm_final)
    in_specs = [pl.BlockSpec((1, tm, D), row), pl.BlockSpec((1, tm, D), row), pl.BlockSpec((1, tm, D), row),
                pl.BlockSpec((1, tm, 2 * D), row), pl.BlockSpec((1, FFN_HALO, d_up), per_batch)]
    in_specs += [_const_spec(w.shape) for w in weights]
    out_shape = [jax.ShapeDtypeStruct((B, T, D), F32), jax.ShapeDtypeStruct((B, FFN_HALO, d_up), F32)]
    out_specs = [pl.BlockSpec((1, tm, D), row), pl.BlockSpec((1, FFN_HALO, d_up), per_batch)]
    return pl.pallas_call(
        functools.partial(_outffn_kernel, tm=tm, final=final),
        out_shape=out_shape, grid=(B, T // tm), in_specs=in_specs, out_specs=out_specs,
        scratch_shapes=[pltpu.VMEM((4 * FFN_COLS // LANES, tm + FFN_HALO, LANES), F32),
                        pltpu.VMEM((FFN_HALO, d_up), F32),
                        pltpu.VMEM((D // LANES, tm, LANES), F32)],
        compiler_params=_params(2), name="outffn_final" if final else "outffn")(x, ap, ca, sg, prev, *weights)


def _swap_halves(w):
    half = w.shape[-1] // 2
    return jnp.concatenate([w[..., half:], w[..., :half]], axis=-1)


def _layer_weights(norm_mix, w_in, q_norm, w_qb, kv_norm, w_kvb, w_o_attn, conv_dw, conv_dw_b, conv_ln_g,
                   conv_ln_b, w_conv_out, w_out, norm_ffn, w_up, ffn_dw, ffn_dw_b, w_down):
    H = N_HEADS
    o0 = Q_LORA_RANK
    o1 = o0 + KV_LORA_RANK
    o2 = o1 + QK_ROPE_DIM
    w_kpe = w_in[:, o1:o2]
    w_qb_h = w_qb.reshape(Q_LORA_RANK, H, QK_DIM)
    w_q_rope = w_qb_h[..., QK_NOPE_DIM:]
    w_kvb_h = w_kvb.reshape(KV_LORA_RANK, H, QK_NOPE_DIM + V_HEAD_DIM)
    w_uk = w_kvb_h[..., :QK_NOPE_DIM]
    w_uv = w_kvb_h[..., QK_NOPE_DIM:]
    row = lambda v: v.reshape(1, -1)
    return dict(
        norm_mix=row(norm_mix), q_norm=row(q_norm), kv_norm=row(kv_norm), norm_ffn=row(norm_ffn),
        wqk=jnp.concatenate([w_in[:, :o0], w_kpe, _swap_halves(w_kpe)], axis=-1).astype(BF16),
        wkva=w_in[:, o0:o1].astype(BF16),
        wcg=w_in[:, o2:].astype(BF16),
        wqn=w_qb_h[..., :QK_NOPE_DIM].reshape(Q_LORA_RANK, H * QK_NOPE_DIM).astype(BF16),
        wqr=jnp.concatenate([w_q_rope, _swap_halves(w_q_rope)], axis=-1).reshape(Q_LORA_RANK, H * LANES).astype(BF16),
        wuk=jnp.transpose(w_uk, (1, 2, 0)).astype(BF16),
        wuv=jnp.transpose(w_uv, (1, 0, 2)).astype(BF16),
        wukt=jnp.transpose(w_uk, (1, 2, 0)).reshape(H * QK_NOPE_DIM, KV_LORA_RANK).astype(BF16),
        wuv_all=w_uv.reshape(KV_LORA_RANK, H * V_HEAD_DIM).astype(BF16),
        woa=w_o_attn.astype(BF16), wco=w_conv_out.astype(BF16), wout=w_out.astype(BF16),
        conv_dw=conv_dw, conv_dw_b=row(conv_dw_b), conv_ln_g=row(conv_ln_g), conv_ln_b=row(conv_ln_b),
        wup=w_up.astype(BF16), ffn_dw=ffn_dw, ffn_dw_b=row(ffn_dw_b), wdn=w_down.astype(BF16))


def _rope_table(pos):
    half = QK_ROPE_DIM // 2
    inv = jnp.power(ROPE_THETA, -jnp.arange(half, dtype=F32) / half)
    ang = pos.astype(F32)[:, None] * inv[None, :]
    cos, sin = jnp.cos(ang), jnp.sin(ang)
    return jnp.concatenate([cos, cos, -sin, sin], axis=-1)


def _pad_front(state, rows):
    B, n, C = state.shape
    return jnp.concatenate([jnp.zeros((B, rows - n, C), state.dtype), state], axis=1)


def _layer(x, tab, cache, conv_prev, ffn_prev, lw, norm_final):
    B, T, D = x.shape
    tile = min(T, TILE_ROWS)
    assert T % tile == 0
    if cache is None:
        q, kt, v, ckv, kpe, u, sg = _inproj(x, tab, lw, tile, latent=False)
        ap = _attn(q, kt, v)
    else:
        qlat, qpe, ckv, ckvb, kpe, kpeb, u, sg = _inproj(x, tab, lw, tile, latent=True)
        ap = _attn_cache(qlat, qpe, cache[0], cache[1], ckvb, kpeb, lw["wuv"])
    ca = _conv(u, _pad_front(conv_prev, CONV_HALO), lw, tile)
    outs = _outffn(x, ap, ca, sg, _pad_front(ffn_prev, FFN_HALO), lw, norm_final, tile)
    conv_state = jnp.concatenate([conv_prev, u], axis=1)[:, -(CONV_WIDTH - 1):]
    ffn_state = outs[1][:, -(FFN_CONV_WIDTH - 1):]
    return outs[0], ckv, kpe, conv_state, ffn_state


def kernel(x_prompt, x_sample, cache_kv_latent, cache_k_rope, state_conv, state_ffn_conv, norm_mix, w_in, q_norm,
           w_qb, kv_norm, w_kvb, w_o_attn, conv_dw, conv_dw_b, conv_ln_g, conv_ln_b, w_conv_out, w_out, norm_ffn,
           w_up, ffn_dw, ffn_dw_b, w_down, norm_final):
    depth = w_in.shape[0]
    Bp, Tp, D = x_prompt.shape
    Bs, Ts, _ = x_sample.shape
    past = cache_kv_latent.shape[2]
    tab_p = _rope_table(jnp.arange(Tp))
    tab_s = _rope_table(past + jnp.arange(Ts))
    per_layer = [norm_mix, w_in, q_norm, w_qb, kv_norm, w_kvb, w_o_attn, conv_dw, conv_dw_b, conv_ln_g, conv_ln_b,
                 w_conv_out, w_out, norm_ffn, w_up, ffn_dw, ffn_dw_b, w_down]
    nfin = norm_final.reshape(1, -1)
    xp, xs = x_prompt, x_sample
    outs_p, outs_s = [], []
    for l in range(depth):
        lw = _layer_weights(*[w[l] for w in per_layer])
        last = nfin if l == depth - 1 else None
        zc = jnp.zeros((Bp, CONV_WIDTH - 1, conv_dw.shape[2]), F32)
        zf = jnp.zeros((Bp, FFN_CONV_WIDTH - 1, w_up.shape[2]), F32)
        xp, *sp = _layer(xp, tab_p, None, zc, zf, lw, last)
        xs, *ss = _layer(xs, tab_s, (cache_kv_latent[l], cache_k_rope[l]), state_conv[l], state_ffn_conv[l],
                         lw, last)
        outs_p.append(sp)
        outs_s.append(ss)
    stack = lambda outs, j: jnp.stack([o[j] for o in outs])
    return (xp, xs,
            stack(outs_p, 0), stack(outs_p, 1), stack(outs_p, 2), stack(outs_p, 3),
            stack(outs_s, 0), stack(outs_s, 1), stack(outs_s, 2), stack(outs_s, 3))
```
